```python
import math
import jax, jax.numpy as jnp
from jax import lax
import numpy as np

D_MODEL = 2048
BATCH = 4
SEQ = 4096
DEPTH = 4

HEAD_DIM = 128
N_HEADS_TOTAL = D_MODEL // HEAD_DIM
N_HEADS_A = N_HEADS_TOTAL // 2
N_HEADS_B = N_HEADS_TOTAL - N_HEADS_A
N_KV_B = max(1, N_HEADS_B // 4)
WIDTH_A = N_HEADS_A * HEAD_DIM
WIDTH_B = N_HEADS_B * HEAD_DIM
KV_WIDTH_B = N_KV_B * HEAD_DIM
MIX_WIDTH = WIDTH_A + WIDTH_B
IN_WIDTH = 3 * WIDTH_A + WIDTH_B + 2 * KV_WIDTH_B
SPLITS = (WIDTH_A, 2 * WIDTH_A, 3 * WIDTH_A, 3 * WIDTH_A + WIDTH_B,
          3 * WIDTH_A + WIDTH_B + KV_WIDTH_B)
DILATED_PATTERNS = ((128, 1), (512, 4), (2048, 16))
GRID_W = 64
ROPE_THETA = 10000.0
Q_BLOCK = 128
D_FF = 256 * ((8 * D_MODEL + 3 * 256 - 1) // (3 * 256))
N_EXPERTS = 8
TOP_K = 2
EXPERT_D_FF = D_FF
EPS = 1e-6
NEG = -1e30

kernel_name = "hymba_dilated_axial_gqa_moe_encoder"


def rms_norm(x, g):
    xf = x.astype(jnp.float32)
    y = xf * lax.rsqrt(jnp.mean(xf * xf, axis=-1, keepdims=True) + EPS)
    return (y * g.astype(jnp.float32)).astype(x.dtype)


def rope_angles(pos, dim):
    inv = ROPE_THETA ** (-jnp.arange(0, dim, 2, dtype=jnp.float32) / dim)
    ang = pos.astype(jnp.float32)[:, None] * inv[None, :]
    return jnp.cos(ang), jnp.sin(ang)


def apply_rope(x, cos, sin):
    xf = x.astype(jnp.float32)
    half = xf.shape[-1] // 2
    x1, x2 = xf[..., :half], xf[..., half:]
    c, s = cos[None, :, None, :], sin[None, :, None, :]
    return jnp.concatenate([x1 * c - x2 * s, x2 * c + x1 * s], axis=-1).astype(x.dtype)


def banded_attention(q, k, v, half):
    *lead, L, dh = q.shape
    nl = len(lead)
    blk = half
    nb = -(-L // blk)
    Lp = nb * blk
    qb = jnp.pad(q, [(0, 0)] * nl + [(0, Lp - L), (0, 0)]).reshape(*lead, nb, blk, dh)

    def neighbours(t):
        tp = jnp.pad(t, [(0, 0)] * nl + [(blk, Lp - L + blk), (0, 0)]).reshape(*lead, nb + 2, blk, dh)
        return jnp.concatenate([tp[..., :-2, :, :], tp[..., 1:-1, :, :], tp[..., 2:, :, :]], axis=-2)

    kn, vn = neighbours(k), neighbours(v)
    s = jnp.einsum('...nqd,...nkd->...nqk', qb, kn, preferred_element_type=jnp.float32) * (dh ** -0.5)
    qpos = (jnp.arange(nb) * blk)[:, None, None] + jnp.arange(blk)[None, :, None]
    kpos = (jnp.arange(nb) * blk - blk)[:, None, None] + jnp.arange(3 * blk)[None, None, :]
    valid = (jnp.abs(kpos - qpos) <= half) & (kpos >= 0) & (kpos < L)
    s = jnp.where(valid, s, NEG)
    m = jnp.max(s, axis=-1, keepdims=True)
    p = jnp.exp(s - m)
    den = jnp.sum(p, axis=-1, keepdims=True)
    o = jnp.einsum('...nqk,...nkd->...nqd', p.astype(v.dtype), vn,
                   preferred_element_type=jnp.float32) / den
    lse = (m + jnp.log(den))[..., 0]
    o = o.reshape(*lead, Lp, dh)[..., :L, :]
    lse = lse.reshape(*lead, Lp)[..., :L]
    return o, lse


def dilated_attention(q, k, v):
    B, H, S, dh = q.shape
    outs, lses = [], []
    for window, dil in DILATED_PATTERNS:
        half = window // (2 * dil)
        L = S // dil

        def split(t):
            return t.reshape(B, H, L, dil, dh).transpose(0, 1, 3, 2, 4)

        o, lse = banded_attention(split(q), split(k), split(v), half)
        outs.append(o.transpose(0, 1, 3, 2, 4).reshape(B, H, S, dh))
        lses.append(lse.transpose(0, 1, 3, 2).reshape(B, H, S))
    w = jax.nn.softmax(jnp.stack(lses, axis=0), axis=0)
    return jnp.sum(w[..., None] * jnp.stack(outs, axis=0), axis=0)


def gqa_blocked(q, k, v):
    B, S, Hq, dh = q.shape
    Hkv = k.shape[2]
    R = Hq // Hkv
    nqb = S // Q_BLOCK
    qb = q.reshape(B, nqb, Q_BLOCK, Hkv, R, dh).transpose(1, 0, 2, 3, 4, 5)
    scale = dh ** -0.5

    def one_block(qblk):
        s = jnp.einsum('bqgrd,bkgd->bgrqk', qblk, k, preferred_element_type=jnp.float32) * scale
        p = jax.nn.softmax(s, axis=-1)
        return jnp.einsum('bgrqk,bkgd->bqgrd', p.astype(v.dtype), v, preferred_element_type=jnp.float32)

    o = lax.map(one_block, qb)
    return o.transpose(1, 0, 2, 3, 4, 5).reshape(B, S, Hq, dh)


def swiglu(t, wg, wu, wd):
    return jnp.matmul(jax.nn.silu(jnp.matmul(t, wg)) * jnp.matmul(t, wu), wd)


def moe_ffn(h, w_router, wg, wu, wd):
    B, S, D = h.shape
    t = h.reshape(B * S, D)
    logits = jnp.matmul(t, w_router, preferred_element_type=jnp.float32)
    top_v, top_i = lax.top_k(logits, TOP_K)
    gates = jax.nn.softmax(top_v, axis=-1)
    combine = jnp.sum(jax.nn.one_hot(top_i, N_EXPERTS, dtype=jnp.float32) * gates[..., None], axis=1)
    y = jnp.zeros((B * S, D), jnp.float32)
    for e in range(N_EXPERTS):
        y = y + combine[:, e:e + 1] * swiglu(t, wg[e], wu[e], wd[e]).astype(jnp.float32)
    return y.astype(h.dtype).reshape(B, S, D)


def setup_inputs(seed: int = 0) -> dict:
    key = jax.random.key(seed)
    ks = jax.random.split(key, 20)
    n_dense = (DEPTH + 1) // 2
    n_moe = DEPTH // 2

    def nrm(k, shape, scale):
        return jax.random.normal(k, shape, jnp.float32) * scale

    def gain(k, shape):
        return 1.0 + 0.02 * jax.random.normal(k, shape, jnp.float32)

    return {
        "x": jax.random.normal(ks[0], (BATCH, SEQ, D_MODEL), jnp.float32),
        "g_mix": gain(ks[1], (DEPTH, D_MODEL)),
        "w_in": nrm(ks[2], (DEPTH, D_MODEL, IN_WIDTH), D_MODEL ** -0.5),
        "qn_a": gain(ks[3], (DEPTH, HEAD_DIM)),
        "kn_a": gain(ks[4], (DEPTH, HEAD_DIM)),
        "qn_b": gain(ks[5], (DEPTH, HEAD_DIM)),
        "kn_b": gain(ks[6], (DEPTH, HEAD_DIM)),
        "on_a": gain(ks[7], (DEPTH, WIDTH_A)),
        "on_b": gain(ks[8], (DEPTH, WIDTH_B)),
        "w_out": nrm(ks[9], (DEPTH, MIX_WIDTH, D_MODEL), MIX_WIDTH ** -0.5),
        "g_ffn": gain(ks[10], (DEPTH, D_MODEL)),
        "w_gate": nrm(ks[11], (n_dense, D_MODEL, D_FF), D_MODEL ** -0.5),
        "w_up": nrm(ks[12], (n_dense, D_MODEL, D_FF), D_MODEL ** -0.5),
        "w_down": nrm(ks[13], (n_dense, D_FF, D_MODEL), D_FF ** -0.5),
        "w_router": nrm(ks[14], (n_moe, D_MODEL, N_EXPERTS), D_MODEL ** -0.5),
        "we_gate": nrm(ks[15], (n_moe, N_EXPERTS, D_MODEL, EXPERT_D_FF), D_MODEL ** -0.5),
        "we_up": nrm(ks[16], (n_moe, N_EXPERTS, D_MODEL, EXPERT_D_FF), D_MODEL ** -0.5),
        "we_down": nrm(ks[17], (n_moe, N_EXPERTS, EXPERT_D_FF, D_MODEL), EXPERT_D_FF ** -0.5),
    }


def reference(x, g_mix, w_in, qn_a, kn_a, qn_b, kn_b, on_a, on_b, w_out, g_ffn,
              w_gate, w_up, w_down, w_router, we_gate, we_up, we_down):
    B, S, _ = x.shape
    pos = jnp.arange(S)
    cos1, sin1 = rope_angles(pos, HEAD_DIM)
    ROWS = S // GRID_W
    row = jnp.repeat(jnp.arange(ROWS), GRID_W)
    col = jnp.tile(jnp.arange(GRID_W), ROWS)
    axis_dim = HEAD_DIM // 2
    cos_r, sin_r = rope_angles(row, axis_dim)
    cos_c, sin_c = rope_angles(col, axis_dim)

    def axial_rope(t):
        return jnp.concatenate([apply_rope(t[..., :axis_dim], cos_r, sin_r),
                                apply_rope(t[..., axis_dim:], cos_c, sin_c)], axis=-1)

    for l in range(DEPTH):
        h = rms_norm(x, g_mix[l])
        proj = jnp.matmul(h, w_in[l])
        qa, ka, va, qb, kb, vb = jnp.split(proj, SPLITS, axis=-1)
        qa = qa.reshape(B, S, N_HEADS_A, HEAD_DIM)
        ka = ka.reshape(B, S, N_HEADS_A, HEAD_DIM)
        va = va.reshape(B, S, N_HEADS_A, HEAD_DIM)
        qa = apply_rope(rms_norm(qa, qn_a[l]), cos1, sin1)
        ka = apply_rope(rms_norm(ka, kn_a[l]), cos1, sin1)
        oa = dilated_attention(qa.transpose(0, 2, 1, 3), ka.transpose(0, 2, 1, 3), va.transpose(0, 2, 1, 3))
        oa = oa.transpose(0, 2, 1, 3).reshape(B, S, WIDTH_A).astype(x.dtype)

        qb = qb.reshape(B, S, N_HEADS_B, HEAD_DIM)
        kb = kb.reshape(B, S, N_KV_B, HEAD_DIM)
        vb = vb.reshape(B, S, N_KV_B, HEAD_DIM)
        qb = axial_rope(rms_norm(qb, qn_b[l]))
        kb = axial_rope(rms_norm(kb, kn_b[l]))
        ob = gqa_blocked(qb, kb, vb).reshape(B, S, WIDTH_B).astype(x.dtype)

        mixed = jnp.concatenate([rms_norm(oa, on_a[l]), rms_norm(ob, on_b[l])], axis=-1)
        x = x + jnp.matmul(mixed, w_out[l])

        h = rms_norm(x, g_ffn[l])
        if l % 2 == 0:
            i = l // 2
            x = x + swiglu(h, w_gate[i], w_up[i], w_down[i])
        else:
            i = l // 2
            x = x + moe_ffn(h, w_router[i], we_gate[i], we_up[i], we_down[i])
    return x
```

```python
import functools

import jax
import jax.numpy as jnp
from jax import lax
from jax.experimental import pallas as pl
from jax.experimental.pallas import tpu as pltpu

HEAD_DIM = 128
N_HEADS_A = 8
N_HEADS_B = 8
N_KV_B = 2
GQA_REP = N_HEADS_B // N_KV_B
N_PROJ_HEADS = 3 * N_HEADS_A + N_HEADS_B + 2 * N_KV_B
DILATED_PATTERNS = ((128, 1), (512, 4), (2048, 16))
BAND_HALF = 64
GRID_W = 64
ROPE_THETA = 10000.0
N_EXPERTS = 8
EPS = 1e-6
NEG = -1e30

HEADS_PER_STEP = 4
LANES = 128
V7X_VMEM_LIMIT_BYTES = 56 * 1024 * 1024

BF16 = jnp.bfloat16
F32 = jnp.float32


def _params(*semantics):
    return pltpu.CompilerParams(dimension_semantics=semantics, vmem_limit_bytes=V7X_VMEM_LIMIT_BYTES)


def _pick(n, pref):
    b = min(n, pref)
    assert n % b == 0, (n, pref)
    return b


def _rope_angles(pos, dim):
    inv = ROPE_THETA ** (-jnp.arange(0, dim, 2, dtype=F32) / dim)
    ang = pos.astype(F32)[:, None] * inv[None, :]
    return jnp.cos(ang), jnp.sin(ang)


def _rope_tables(seq):
    pos = jnp.arange(seq)
    c1, s1 = _rope_angles(pos, HEAD_DIM)
    cos_a = jnp.concatenate([c1, c1], axis=-1)
    sin_a = jnp.concatenate([-s1, s1], axis=-1)
    cr, sr = _rope_angles(pos // GRID_W, HEAD_DIM // 2)
    cc, sc = _rope_angles(pos % GRID_W, HEAD_DIM // 2)
    z = jnp.zeros_like(sr)
    cos_b = jnp.concatenate([cr, cr, cc, cc], axis=-1)
    sin_b_up = jnp.concatenate([-sr, z, -sc, z], axis=-1)
    sin_b_dn = jnp.concatenate([z, sr, z, sc], axis=-1)
    return cos_a, sin_a, cos_b, sin_b_up, sin_b_dn


def _inproj_kernel(x_ref, g_ref, w_ref, gain_ref, cosa_ref, sina_ref, cosb_ref, sinbu_ref, sinbd_ref,
                   out_ref, hn_ref):
    j = pl.program_id(1)

    @pl.when(j == 0)
    def _():
        x = x_ref[...]
        ms = jnp.mean(x * x, axis=-1, keepdims=True)
        hn_ref[...] = (x * lax.rsqrt(ms + EPS) * g_ref[...]).astype(BF16)

    y = jnp.dot(hn_ref[...], w_ref[...], preferred_element_type=F32)

    def head(hh):
        return y[:, hh * HEAD_DIM:(hh + 1) * HEAD_DIM]

    def normed(hh):
        yh = head(hh)
        ms = jnp.mean(yh * yh, axis=-1, keepdims=True)
        return yh * lax.rsqrt(ms + EPS) * gain_ref[:, hh * HEAD_DIM:(hh + 1) * HEAD_DIM]

    def rope_a(hh):
        yn = normed(hh)
        return yn * cosa_ref[...] + pltpu.roll(yn, 64, 1) * sina_ref[...]

    def rope_b(hh):
        yn = normed(hh)
        return (yn * cosb_ref[...] + pltpu.roll(yn, 96, 1) * sinbu_ref[...]
                + pltpu.roll(yn, 32, 1) * sinbd_ref[...])

    @pl.when(j < 4)
    def _():
        for hh in range(HEADS_PER_STEP):
            out_ref[hh] = rope_a(hh).astype(BF16)

    @pl.when((j >= 4) & (j < 6))
    def _():
        for hh in range(HEADS_PER_STEP):
            out_ref[hh] = head(hh).astype(BF16)

    @pl.when((j >= 6) & (j < 8))
    def _():
        for hh in range(HEADS_PER_STEP):
            out_ref[hh] = rope_b(hh).astype(BF16)

    @pl.when(j == 8)
    def _():
        for hh in range(N_KV_B):
            out_ref[hh] = rope_b(hh).astype(BF16)
        for hh in range(N_KV_B, 2 * N_KV_B):
            out_ref[hh] = head(hh).astype(BF16)


def _inproj(x2, g, w, gain_cols, tables, seq):
    t, d = x2.shape
    bm = _pick(seq, 512)
    n_pos_blocks = seq // bm
    bn = HEADS_PER_STEP * HEAD_DIM
    nj = N_PROJ_HEADS // HEADS_PER_STEP
    tab_spec = pl.BlockSpec((bm, HEAD_DIM), lambda i, j: (i % n_pos_blocks, 0))
    return pl.pallas_call(
        _inproj_kernel,
        grid=(t // bm, nj),
        in_specs=[
            pl.BlockSpec((bm, d), lambda i, j: (i, 0)),
            pl.BlockSpec((1, d), lambda i, j: (0, 0)),
            pl.BlockSpec((d, bn), lambda i, j: (0, j)),
            pl.BlockSpec((1, bn), lambda i, j: (0, j)),
            tab_spec, tab_spec, tab_spec, tab_spec, tab_spec,
        ],
        out_specs=pl.BlockSpec((HEADS_PER_STEP, bm, HEAD_DIM), lambda i, j: (j, i, 0)),
        out_shape=jax.ShapeDtypeStruct((N_PROJ_HEADS, t, HEAD_DIM), BF16),
        scratch_shapes=[pltpu.VMEM((bm, d), BF16)],
        compiler_params=_params("parallel", "arbitrary"),
        name="inproj",
    )(x2, g, w, gain_cols, *tables)


QB_DIL = 128
KW_DIL = QB_DIL + 2 * BAND_HALF


def _dilated_kernel(q_ref, k_ref, v_ref, o_ref, lse_ref, *, dil, rows):
    n_blocks = rows // QB_DIL
    rel0 = (lax.broadcasted_iota(jnp.int32, (QB_DIL, KW_DIL), 1)
            - lax.broadcasted_iota(jnp.int32, (QB_DIL, KW_DIL), 0))

    def block(r, lb):
        cols = slice(r * HEAD_DIM, (r + 1) * HEAD_DIM)
        l0 = pl.multiple_of(lb * QB_DIL, QB_DIL)
        w0 = pl.multiple_of(jnp.clip(l0 - BAND_HALF, 0, rows - KW_DIL), BAND_HALF)
        q = q_ref[pl.ds(l0, QB_DIL), cols]
        k = k_ref[pl.ds(w0, KW_DIL), cols]
        v = v_ref[pl.ds(w0, KW_DIL), cols]
        s = lax.dot_general(q, k, (((1,), (1,)), ((), ())), preferred_element_type=F32)
        rel = rel0 + (w0 - l0)
        s = jnp.where(jnp.abs(rel) <= BAND_HALF, s, NEG)
        m = jnp.max(s, axis=-1, keepdims=True)
        p = jnp.exp(s - m)
        den = jnp.sum(p, axis=-1, keepdims=True)
        o = jnp.dot(p.astype(BF16), v, preferred_element_type=F32) / den
        o_ref[pl.ds(l0, QB_DIL), cols] = o.astype(BF16)
        lse_ref[pl.ds(l0, QB_DIL), cols] = jnp.broadcast_to(m + jnp.log(den), (QB_DIL, HEAD_DIM))

    for r in range(dil):
        def body(lb, carry, r=r):
            block(r, lb)
            return carry
        lax.fori_loop(0, n_blocks, body, 0)


def _dilated_branch(projh, batch, seq, dil):
    rows = seq // dil
    width = dil * HEAD_DIM
    assert rows >= KW_DIL and rows % QB_DIL == 0
    view = projh.reshape(N_PROJ_HEADS, batch, rows, width)

    def spec(off):
        return pl.BlockSpec((None, None, rows, width), lambda b, h: (off + h, b, 0, 0))

    out_spec = pl.BlockSpec((None, None, rows, width), lambda b, h: (h, b, 0, 0))
    o, lse = pl.pallas_call(
        functools.partial(_dilated_kernel, dil=dil, rows=rows),
        grid=(batch, N_HEADS_A),
        in_specs=[spec(0), spec(N_HEADS_A), spec(2 * N_HEADS_A)],
        out_specs=[out_spec, out_spec],
        out_shape=[jax.ShapeDtypeStruct((N_HEADS_A, batch, rows, width), BF16),
                   jax.ShapeDtypeStruct((N_HEADS_A, batch, rows, width), F32)],
        compiler_params=_params("parallel", "parallel"),
        name=f"dilated_d{dil}",
    )(view, view, view)
    t = batch * seq
    return o.reshape(N_HEADS_A, t, HEAD_DIM), lse.reshape(N_HEADS_A, t, HEAD_DIM)


def _gqa_kernel(q_ref, k_ref, v_ref, o_ref, *, bk):
    rep, bq, dh = q_ref.shape
    seq = k_ref.shape[0]
    q = q_ref[...].reshape(rep * bq, dh)

    def body(c, carry):
        m, l, acc = carry
        c0 = pl.multiple_of(c * bk, bk)
        k = k_ref[pl.ds(c0, bk), :]
        v = v_ref[pl.ds(c0, bk), :]
        s = lax.dot_general(q, k, (((1,), (1,)), ((), ())), preferred_element_type=F32)
        m_new = jnp.maximum(m, jnp.max(s, axis=-1, keepdims=True))
        alpha = jnp.exp(m - m_new)
        p = jnp.exp(s - m_new)
        l = alpha * l + jnp.sum(p, axis=-1, keepdims=True)
        acc = alpha * acc + jnp.dot(p.astype(BF16), v, preferred_element_type=F32)
        return m_new, l, acc

    init = (jnp.full((rep * bq, 1), NEG, F32), jnp.zeros((rep * bq, 1), F32),
            jnp.zeros((rep * bq, dh), F32))
    _, l, acc = lax.fori_loop(0, seq // bk, body, init)
    o = acc / l
    for h in range(rep):
        o_ref[:, h * dh:(h + 1) * dh] = o[h * bq:(h + 1) * bq].astype(o_ref.dtype)


def _gqa(projh, batch, seq):
    t = batch * seq
    bq = _pick(seq, 256)
    bk = _pick(seq, 512)
    nq = seq // bq
    view = projh.reshape(N_PROJ_HEADS, batch, seq, HEAD_DIM)
    q_blk0 = 3 * N_HEADS_A // GQA_REP
    k_head0 = 3 * N_HEADS_A + N_HEADS_B
    v_head0 = k_head0 + N_KV_B
    return pl.pallas_call(
        functools.partial(_gqa_kernel, bk=bk),
        grid=(batch, N_KV_B, nq),
        in_specs=[
            pl.BlockSpec((GQA_REP, None, bq, HEAD_DIM), lambda b, g, i: (q_blk0 + g, b, i, 0)),
            pl.BlockSpec((None, None, seq, HEAD_DIM), lambda b, g, i: (k_head0 + g, b, 0, 0)),
            pl.BlockSpec((None, None, seq, HEAD_DIM), lambda b, g, i: (v_head0 + g, b, 0, 0)),
        ],
        out_specs=pl.BlockSpec((bq, GQA_REP * HEAD_DIM), lambda b, g, i: (b * nq + i, g)),
        out_shape=jax.ShapeDtypeStruct((t, N_HEADS_B * HEAD_DIM), BF16),
        compiler_params=_params("parallel", "parallel", "arbitrary"),
        name="gqa",
    )(view, view, view)


def _outproj_kernel(o1_ref, o2_ref, o3_ref, l1_ref, l2_ref, l3_ref, ob_ref, ona_ref, onb_ref,
                    w_ref, x_ref, out_ref, mixed_ref, oa_ref):
    j = pl.program_id(1)
    wa = N_HEADS_A * HEAD_DIM
    wb = ob_ref.shape[1]

    @pl.when(j == 0)
    def _():
        ss = jnp.zeros((oa_ref.shape[0], HEAD_DIM), F32)
        for h in range(N_HEADS_A):
            l1, l2, l3 = l1_ref[h], l2_ref[h], l3_ref[h]
            mx = jnp.maximum(jnp.maximum(l1, l2), l3)
            e1, e2, e3 = jnp.exp(l1 - mx), jnp.exp(l2 - mx), jnp.exp(l3 - mx)
            oa = (e1 * o1_ref[h].astype(F32) + e2 * o2_ref[h].astype(F32)
                  + e3 * o3_ref[h].astype(F32)) / (e1 + e2 + e3)
            oa_ref[:, h * HEAD_DIM:(h + 1) * HEAD_DIM] = oa
            ss = ss + oa * oa
        ra = lax.rsqrt(jnp.sum(ss, axis=-1, keepdims=True) / wa + EPS)
        mixed_ref[:, :wa] = (oa_ref[...] * ra * ona_ref[...]).astype(BF16)
        ob = ob_ref[...].astype(F32)
        rb = lax.rsqrt(jnp.mean(ob * ob, axis=-1, keepdims=True) + EPS)
        mixed_ref[:, wa:wa + wb] = (ob * rb * onb_ref[...]).astype(BF16)

    out_ref[...] = x_ref[...] + jnp.dot(mixed_ref[...], w_ref[...], preferred_element_type=F32)


def _outproj(o_branches, lse_branches, ob, on_a, on_b, w, x2):
    t, d = x2.shape
    bm = _pick(t, 512)
    bn = _pick(d, 512)
    wa = N_HEADS_A * HEAD_DIM
    wb = ob.shape[1]
    hm_spec = pl.BlockSpec((N_HEADS_A, bm, HEAD_DIM), lambda i, j: (0, i, 0))
    return pl.pallas_call(
        _outproj_kernel,
        grid=(t // bm, d // bn),
        in_specs=[
            hm_spec, hm_spec, hm_spec, hm_spec, hm_spec, hm_spec,
            pl.BlockSpec((bm, wb), lambda i, j: (i, 0)),
            pl.BlockSpec((1, wa), lambda i, j: (0, 0)),
            pl.BlockSpec((1, wb), lambda i, j: (0, 0)),
            pl.BlockSpec((wa + wb, bn), lambda i, j: (0, j)),
            pl.BlockSpec((bm, bn), lambda i, j: (i, j)),
        ],
        out_specs=pl.BlockSpec((bm, bn), lambda i, j: (i, j)),
        out_shape=jax.ShapeDtypeStruct((t, d), F32),
        scratch_shapes=[pltpu.VMEM((bm, wa + wb), BF16), pltpu.VMEM((bm, wa), F32)],
        compiler_params=_params("parallel", "arbitrary"),
        name="outproj",
    )(*o_branches, *lse_branches, ob, on_a, on_b, w, x2)


def _swiglu_step(h, wg_ref, wu_ref, wd_ref, acc_ref, j):
    g = jnp.dot(h, wg_ref[...], preferred_element_type=F32)
    u = jnp.dot(h, wu_ref[...], preferred_element_type=F32)
    a = (g * jax.nn.sigmoid(g) * u).astype(BF16)
    part = jnp.dot(a, wd_ref[...], preferred_element_type=F32)

    @pl.when(j == 0)
    def _():
        acc_ref[...] = part

    @pl.when(j > 0)
    def _():
        acc_ref[...] += part


def _dense_ffn_kernel(x_ref, g_ref, wg_ref, wu_ref, wd_ref, out_ref, hn_ref, acc_ref):
    j = pl.program_id(1)

    @pl.when(j == 0)
    def _():
        x = x_ref[...]
        ms = jnp.mean(x * x, axis=-1, keepdims=True)
        hn_ref[...] = (x * lax.rsqrt(ms + EPS) * g_ref[...]).astype(BF16)

    _swiglu_step(hn_ref[...], wg_ref, wu_ref, wd_ref, acc_ref, j)

    @pl.when(j == pl.num_programs(1) - 1)
    def _():
        out_ref[...] = x_ref[...] + acc_ref[...]


def _dense_ffn(x2, g, wg, wu, wd):
    t, d = x2.shape
    f = wg.shape[1]
    bm = _pick(t, 512)
    bf = _pick(f, 512)
    return pl.pallas_call(
        _dense_ffn_kernel,
        grid=(t // bm, f // bf),
        in_specs=[
            pl.BlockSpec((bm, d), lambda i, j: (i, 0)),
            pl.BlockSpec((1, d), lambda i, j: (0, 0)),
            pl.BlockSpec((d, bf), lambda i, j: (0, j)),
            pl.BlockSpec((d, bf), lambda i, j: (0, j)),
            pl.BlockSpec((bf, d), lambda i, j: (j, 0)),
        ],
        out_specs=pl.BlockSpec((bm, d), lambda i, j: (i, 0)),
        out_shape=jax.ShapeDtypeStruct((t, d), F32),
        scratch_shapes=[pltpu.VMEM((bm, d), BF16), pltpu.VMEM((bm, d), F32)],
        compiler_params=_params("parallel", "arbitrary"),
        name="dense_ffn",
    )(x2, g, wg, wu, wd)


INFO_E1, INFO_E2, INFO_R1, INFO_R2, INFO_G1, INFO_G2 = range(6)


def _router_kernel(x_ref, g_ref, wr_ref, hn_ref, info_ref, cnt_ref, run_ref):
    i = pl.program_id(0)
    bm = x_ref.shape[0]

    @pl.when(i == 0)
    def _():
        run_ref[...] = jnp.zeros_like(run_ref)

    x = x_ref[...]
    ms = jnp.mean(x * x, axis=-1, keepdims=True)
    hn = x * lax.rsqrt(ms + EPS) * g_ref[...]
    hn_ref[...] = hn
    logits = jnp.dot(hn, wr_ref[...], preferred_element_type=F32, precision=lax.Precision.HIGHEST)
    lane = lax.broadcasted_iota(jnp.int32, (bm, LANES), 1)
    lg = jnp.where(lane < N_EXPERTS, logits, -jnp.inf)
    v1 = jnp.max(lg, axis=-1, keepdims=True)
    i1 = jnp.min(jnp.where(lg == v1, lane, LANES), axis=-1, keepdims=True)
    lg2 = jnp.where(lane == i1, -jnp.inf, lg)
    v2 = jnp.max(lg2, axis=-1, keepdims=True)
    i2 = jnp.min(jnp.where(lg2 == v2, lane, LANES), axis=-1, keepdims=True)
    e = jnp.exp(v2 - v1)
    g1 = 1.0 / (1.0 + e)
    g2 = e / (1.0 + e)
    hot1 = lane == i1
    hot2 = lane == i2
    onehot = (hot1 | hot2).astype(F32)
    tri = (lax.broadcasted_iota(jnp.int32, (bm, bm), 1)
           < lax.broadcasted_iota(jnp.int32, (bm, bm), 0)).astype(BF16)
    rank = jnp.dot(tri, onehot.astype(BF16), preferred_element_type=F32) + run_ref[0:1, :]
    r1 = jnp.sum(jnp.where(hot1, rank, 0.0), axis=-1, keepdims=True)
    r2 = jnp.sum(jnp.where(hot2, rank, 0.0), axis=-1, keepdims=True)
    run_ref[...] = run_ref[...] + jnp.sum(onehot, axis=0, keepdims=True)
    info = jnp.zeros((bm, LANES), F32)
    for col, val in ((INFO_E1, i1.astype(F32)), (INFO_E2, i2.astype(F32)), (INFO_R1, r1),
                     (INFO_R2, r2), (INFO_G1, g1), (INFO_G2, g2)):
        info = jnp.where(lane == col, val, info)
    info_ref[...] = info
    cnt_ref[...] = run_ref[...]


def _router(x2, g, wr_pad):
    t, d = x2.shape
    bm = _pick(t, 512)
    return pl.pallas_call(
        _router_kernel,
        grid=(t // bm,),
        in_specs=[
            pl.BlockSpec((bm, d), lambda i: (i, 0)),
            pl.BlockSpec((1, d), lambda i: (0, 0)),
            pl.BlockSpec((d, LANES), lambda i: (0, 0)),
        ],
        out_specs=[
            pl.BlockSpec((bm, d), lambda i: (i, 0)),
            pl.BlockSpec((bm, LANES), lambda i: (i, 0)),
            pl.BlockSpec((8, LANES), lambda i: (0, 0)),
        ],
        out_shape=[jax.ShapeDtypeStruct((t, d), F32), jax.ShapeDtypeStruct((t, LANES), F32),
                   jax.ShapeDtypeStruct((8, LANES), F32)],
        scratch_shapes=[pltpu.VMEM((8, LANES), F32)],
        compiler_params=_params("arbitrary"),
        name="router",
    )(x2, g, wr_pad)


def _scatter_kernel(pos1_ref, pos2_ref, hn_ref, hs_in_ref, hs_ref, sem):
    del hs_in_ref
    i = pl.program_id(0)
    bt = pos1_ref.shape[-1]

    def copies(k):
        src = hn_ref.at[pl.ds(i * bt + k, 1), :]
        return (pltpu.make_async_copy(src, hs_ref.at[pl.ds(pos1_ref[0, 0, k], 1), :], sem.at[0]),
                pltpu.make_async_copy(src, hs_ref.at[pl.ds(pos2_ref[0, 0, k], 1), :], sem.at[1]))

    def start(k, c):
        a, b = copies(k)
        a.start()
        b.start()
        return c

    def wait(k, c):
        a, b = copies(k)
        a.wait()
        b.wait()
        return c

    lax.fori_loop(0, bt, start, 0)
    lax.fori_loop(0, bt, wait, 0)


def _scatter_rows(hn, pos1, pos2, n_rows):
    t, d = hn.shape
    bt = _pick(t, 512)
    smem_spec = pl.BlockSpec((1, 1, bt), lambda i: (i, 0, 0), memory_space=pltpu.SMEM)
    any_spec = pl.BlockSpec(memory_space=pl.ANY)
    return pl.pallas_call(
        _scatter_kernel,
        grid=(t // bt,),
        in_specs=[smem_spec, smem_spec, any_spec, any_spec],
        out_specs=any_spec,
        out_shape=jax.ShapeDtypeStruct((n_rows, d), hn.dtype),
        scratch_shapes=[pltpu.SemaphoreType.DMA((2,))],
        input_output_aliases={3: 0},
        compiler_params=_params("arbitrary"),
        name="scatter_rows",
    )(pos1.reshape(t // bt, 1, bt), pos2.reshape(t // bt, 1, bt), hn, jnp.zeros((n_rows, d), hn.dtype))


def _grouped_ffn_kernel(be_ref, na_ref, hs_ref, wg_ref, wu_ref, wd_ref, out_ref, hb_ref, acc_ref):
    del be_ref
    i = pl.program_id(0)
    j = pl.program_id(1)
    active = i < na_ref[0]

    @pl.when(active & (j == 0))
    def _():
        hb_ref[...] = hs_ref[...].astype(BF16)

    @pl.when(active)
    def _():
        _swiglu_step(hb_ref[...], wg_ref, wu_ref, wd_ref, acc_ref, j)

    @pl.when(j == pl.num_programs(1) - 1)
    def _():
        @pl.when(active)
        def _():
            out_ref[...] = acc_ref[...]

        @pl.when(jnp.logical_not(active))
        def _():
            out_ref[...] = jnp.zeros_like(out_ref)


def _grouped_ffn(hs, block_expert, n_active, wg, wu, wd, bm):
    p, d = hs.shape
    f = wg.shape[2]
    bf = _pick(f, 512)
    nf = f // bf

    def jj(i, j, na):
        return jnp.where(i < na[0], j, nf - 1)

    grid_spec = pltpu.PrefetchScalarGridSpec(
        num_scalar_prefetch=2,
        grid=(p // bm, nf),
        in_specs=[
            pl.BlockSpec((bm, d), lambda i, j, be, na: (jnp.minimum(i, na[0] - 1), 0)),
            pl.BlockSpec((None, d, bf), lambda i, j, be, na: (be[i], 0, jj(i, j, na))),
            pl.BlockSpec((None, d, bf), lambda i, j, be, na: (be[i], 0, jj(i, j, na))),
            pl.BlockSpec((None, bf, d), lambda i, j, be, na: (be[i], jj(i, j, na), 0)),
        ],
        out_specs=pl.BlockSpec((bm, d), lambda i, j, be, na: (i, 0)),
        scratch_shapes=[pltpu.VMEM((bm, d), BF16), pltpu.VMEM((bm, d), F32)],
    )
    return pl.pallas_call(
        _grouped_ffn_kernel,
        grid_spec=grid_spec,
        out_shape=jax.ShapeDtypeStruct((p, d), F32),
        compiler_params=_params("arbitrary", "arbitrary"),
        name="grouped_ffn",
    )(block_expert, n_active, hs, wg, wu, wd)


def _combine_kernel(pos1_ref, pos2_ref, x_ref, info_ref, y_ref, out_ref, buf1_ref, buf2_ref, sem):
    bt = x_ref.shape[0]

    def copies(k):
        return (pltpu.make_async_copy(y_ref.at[pl.ds(pos1_ref[0, 0, k], 1), :],
                                      buf1_ref.at[pl.ds(k, 1), :], sem.at[0]),
                pltpu.make_async_copy(y_ref.at[pl.ds(pos2_ref[0, 0, k], 1), :],
                                      buf2_ref.at[pl.ds(k, 1), :], sem.at[1]))

    def start(k, c):
        a, b = copies(k)
        a.start()
        b.start()
        return c

    def wait(k, c):
        a, b = copies(k)
        a.wait()
        b.wait()
        return c

    lax.fori_loop(0, bt, start, 0)
    lax.fori_loop(0, bt, wait, 0)
    info = info_ref[...]
    g1 = info[:, INFO_G1:INFO_G1 + 1]
    g2 = info[:, INFO_G2:INFO_G2 + 1]
    out_ref[...] = x_ref[...] + g1 * buf1_ref[...] + g2 * buf2_ref[...]


def _combine(x2, info, y, pos1, pos2):
    t, d = x2.shape
    bt = _pick(t, 256)
    smem_spec = pl.BlockSpec((1, 1, bt), lambda i: (i, 0, 0), memory_space=pltpu.SMEM)
    return pl.pallas_call(
        _combine_kernel,
        grid=(t // bt,),
        in_specs=[
            smem_spec, smem_spec,
            pl.BlockSpec((bt, d), lambda i: (i, 0)),
            pl.BlockSpec((bt, LANES), lambda i: (i, 0)),
            pl.BlockSpec(memory_space=pl.ANY),
        ],
        out_specs=pl.BlockSpec((bt, d), lambda i: (i, 0)),
        out_shape=jax.ShapeDtypeStruct((t, d), F32),
        scratch_shapes=[pltpu.VMEM((bt, d), F32), pltpu.VMEM((bt, d), F32),
                        pltpu.SemaphoreType.DMA((2,))],
        compiler_params=_params("arbitrary"),
        name="combine",
    )(pos1.reshape(t // bt, 1, bt), pos2.reshape(t // bt, 1, bt), x2, info, y)


MOE_BLOCK_ROWS = 512


def _moe_ffn(x2, g, wr, wg, wu, wd):
    t, d = x2.shape
    bm = MOE_BLOCK_ROWS
    wr_pad = jnp.zeros((d, LANES), F32).at[:, :N_EXPERTS].set(wr)
    hn, info, cnt = _router(x2, g, wr_pad)
    e1 = info[:, INFO_E1].astype(jnp.int32)
    e2 = info[:, INFO_E2].astype(jnp.int32)
    r1 = info[:, INFO_R1].astype(jnp.int32)
    r2 = info[:, INFO_R2].astype(jnp.int32)
    counts = cnt[0, :N_EXPERTS].astype(jnp.int32)
    blocks_per_expert = (counts + bm - 1) // bm
    block_end = jnp.cumsum(blocks_per_expert)
    offsets = (block_end - blocks_per_expert) * bm
    pos1 = offsets[e1] + r1
    pos2 = offsets[e2] + r2
    n_blocks = (2 * t) // bm + N_EXPERTS
    n_active = block_end[-1:]
    ids = jnp.minimum(jnp.arange(n_blocks, dtype=jnp.int32), n_active[0] - 1)
    block_expert = jnp.minimum(jnp.searchsorted(block_end, ids, side="right"), N_EXPERTS - 1).astype(jnp.int32)
    hs = _scatter_rows(hn, pos1, pos2, n_blocks * bm)
    y = _grouped_ffn(hs, block_expert, n_active.astype(jnp.int32), wg, wu, wd, bm)
    return _combine(x2, info, y, pos1, pos2)


def kernel(x, g_mix, w_in, qn_a, kn_a, qn_b, kn_b, on_a, on_b, w_out, g_ffn, w_gate, w_up, w_down,
           w_router, we_gate, we_up, we_down):
    batch, seq, d = x.shape
    depth = w_in.shape[0]
    t = batch * seq
    assert all(w // (2 * dil) == BAND_HALF for w, dil in DILATED_PATTERNS)
    assert w_in.shape[2] == N_PROJ_HEADS * HEAD_DIM
    scale = HEAD_DIM ** -0.5
    tables = _rope_tables(seq)
    ones_v = jnp.ones((N_HEADS_A * HEAD_DIM,), F32)
    x2 = x.reshape(t, d)
    for l in range(depth):
        gain_cols = jnp.concatenate([
            jnp.tile(qn_a[l] * scale, N_HEADS_A), jnp.tile(kn_a[l], N_HEADS_A), ones_v,
            jnp.tile(qn_b[l] * scale, N_HEADS_B), jnp.tile(kn_b[l], N_KV_B),
            jnp.ones((N_KV_B * HEAD_DIM,), F32)])[None, :]
        projh = _inproj(x2, g_mix[l][None, :], w_in[l].astype(BF16), gain_cols, tables, seq)
        branches = [_dilated_branch(projh, batch, seq, dil) for _, dil in DILATED_PATTERNS]
        ob = _gqa(projh, batch, seq)
        x2 = _outproj([b[0] for b in branches], [b[1] for b in branches], ob, on_a[l][None, :],
                      on_b[l][None, :], w_out[l].astype(BF16), x2)
        i = l // 2
        if l % 2 == 0:
            x2 = _dense_ffn(x2, g_ffn[l][None, :], w_gate[i].astype(BF16), w_up[i].astype(BF16),
                            w_down[i].astype(BF16))
        else:
            x2 = _moe_ffn(x2, g_ffn[l][None, :], w_router[i], we_gate[i].astype(BF16),
                          we_up[i].astype(BF16), we_down[i].astype(BF16))
    return x2.reshape(batch, seq, d)
```

```python
import functools

import jax
import jax.numpy as jnp
from jax import lax
from jax.experimental import pallas as pl
from jax.experimental.pallas import tpu as pltpu

HEAD_DIM = 128
N_HEADS_A = 8
N_HEADS_B = 8
N_KV_B = 2
GQA_REP = N_HEADS_B // N_KV_B
N_COLS_A = 3 * N_HEADS_A
N_COLS_B = N_HEADS_B + 2 * N_KV_B
DILATED_PATTERNS = ((128, 1), (512, 4), (2048, 16))
DILATIONS = tuple(d for _, d in DILATED_PATTERNS)
BAND_HALF = 64
GRID_W = 64
ROPE_THETA = 10000.0
N_EXPERTS = 8
EPS = 1e-6
NEG = -1e30
MAX_UNSHIFTED_SCORE = 40.0

HEADS_PER_STEP = 4
LANES = 128
V7X_VMEM_LIMIT_BYTES = 56 * 1024 * 1024

BF16 = jnp.bfloat16
F32 = jnp.float32


def _params(*semantics):
    return pltpu.CompilerParams(dimension_semantics=semantics, vmem_limit_bytes=V7X_VMEM_LIMIT_BYTES)


def _pick(n, pref):
    b = min(n, pref)
    assert n % b == 0, (n, pref)
    return b


def _rope_angles(pos, dim):
    inv = ROPE_THETA ** (-jnp.arange(0, dim, 2, dtype=F32) / dim)
    ang = pos.astype(F32)[:, None] * inv[None, :]
    return jnp.cos(ang), jnp.sin(ang)


def _rope_tables(seq):
    pos = jnp.arange(seq)
    c1, s1 = _rope_angles(pos, HEAD_DIM)
    cos_a = jnp.concatenate([c1, c1], axis=-1)
    sin_a = jnp.concatenate([-s1, s1], axis=-1)
    cr, sr = _rope_angles(pos // GRID_W, HEAD_DIM // 2)
    cc, sc = _rope_angles(pos % GRID_W, HEAD_DIM // 2)
    z = jnp.zeros_like(sr)
    cos_b = jnp.concatenate([cr, cr, cc, cc], axis=-1)
    sin_b_up = jnp.concatenate([-sr, z, -sc, z], axis=-1)
    sin_b_dn = jnp.concatenate([z, sr, z, sc], axis=-1)
    return (cos_a, sin_a), (cos_b, sin_b_up, sin_b_dn)


def _rms_to_bf16(x_ref, g_ref, hn_ref):
    x = x_ref[...]
    ms = jnp.mean(x * x, axis=-1, keepdims=True)
    hn_ref[...] = (x * lax.rsqrt(ms + EPS) * g_ref[...]).astype(BF16)


def _head(y, hh):
    return y[:, hh * HEAD_DIM:(hh + 1) * HEAD_DIM]


def _normed_head(y, gain_ref, hh):
    yh = _head(y, hh)
    ms = jnp.mean(yh * yh, axis=-1, keepdims=True)
    return yh * lax.rsqrt(ms + EPS) * gain_ref[:, hh * HEAD_DIM:(hh + 1) * HEAD_DIM]


def _inproj_a_kernel(x_ref, g_ref, w_ref, gain_ref, cos_ref, sin_ref, out1_ref, out4_ref, out16_ref,
                     hn_ref, tmp_ref):
    j = pl.program_id(1)
    bm = x_ref.shape[0]

    @pl.when(j == 0)
    def _():
        _rms_to_bf16(x_ref, g_ref, hn_ref)

    y = jnp.dot(hn_ref[...], w_ref[...], preferred_element_type=F32)

    @pl.when(j < 2 * N_HEADS_A // HEADS_PER_STEP)
    def _():
        for hh in range(HEADS_PER_STEP):
            yn = _normed_head(y, gain_ref, hh)
            tmp_ref[hh] = yn * cos_ref[...] + pltpu.roll(yn, 64, 1) * sin_ref[...]

    @pl.when(j >= 2 * N_HEADS_A // HEADS_PER_STEP)
    def _():
        for hh in range(HEADS_PER_STEP):
            tmp_ref[hh] = _head(y, hh)

    for hh in range(HEADS_PER_STEP):
        out1_ref[hh] = tmp_ref[hh].astype(BF16)
        for out_ref, dil in ((out4_ref, 4), (out16_ref, 16)):
            for r in range(dil):
                out_ref[hh, r] = tmp_ref[hh, pl.ds(r, bm // dil, stride=dil), :].astype(BF16)


def _inproj_a(x2, g, w, gain_cols, tables, batch, seq):
    t, d = x2.shape
    bm = _pick(seq, 512)
    nps = seq // bm
    bn = HEADS_PER_STEP * HEAD_DIM
    tab_spec = pl.BlockSpec((bm, HEAD_DIM), lambda i, j: (i % nps, 0))

    def rm_spec(dil):
        return pl.BlockSpec((HEADS_PER_STEP, None, dil, bm // dil, HEAD_DIM),
                            lambda i, j: (j, i // nps, 0, i % nps, 0))

    def rm_shape(dil):
        return jax.ShapeDtypeStruct((N_COLS_A, batch, dil, seq // dil, HEAD_DIM), BF16)

    return pl.pallas_call(
        _inproj_a_kernel,
        grid=(t // bm, N_COLS_A // HEADS_PER_STEP),
        in_specs=[
            pl.BlockSpec((bm, d), lambda i, j: (i, 0)),
            pl.BlockSpec((1, d), lambda i, j: (0, 0)),
            pl.BlockSpec((d, bn), lambda i, j: (0, j)),
            pl.BlockSpec((1, bn), lambda i, j: (0, j)),
            tab_spec, tab_spec,
        ],
        out_specs=[pl.BlockSpec((HEADS_PER_STEP, bm, HEAD_DIM), lambda i, j: (j, i, 0)),
                   rm_spec(4), rm_spec(16)],
        out_shape=[jax.ShapeDtypeStruct((N_COLS_A, t, HEAD_DIM), BF16), rm_shape(4), rm_shape(16)],
        scratch_shapes=[pltpu.VMEM((bm, d), BF16), pltpu.VMEM((HEADS_PER_STEP, bm, HEAD_DIM), F32)],
        compiler_params=_params("parallel", "arbitrary"),
        name="inproj_a",
    )(x2, g, w, gain_cols, *tables)


def _inproj_b_kernel(x_ref, g_ref, w_ref, gain_ref, cos_ref, sinu_ref, sind_ref, out_ref, hn_ref):
    j = pl.program_id(1)

    @pl.when(j == 0)
    def _():
        _rms_to_bf16(x_ref, g_ref, hn_ref)

    y = jnp.dot(hn_ref[...], w_ref[...], preferred_element_type=F32)

    def rope(hh):
        yn = _normed_head(y, gain_ref, hh)
        return (yn * cos_ref[...] + pltpu.roll(yn, 96, 1) * sinu_ref[...]
                + pltpu.roll(yn, 32, 1) * sind_ref[...])

    @pl.when(j < N_HEADS_B // HEADS_PER_STEP)
    def _():
        for hh in range(HEADS_PER_STEP):
            out_ref[hh] = rope(hh).astype(BF16)

    @pl.when(j == N_HEADS_B // HEADS_PER_STEP)
    def _():
        for hh in range(N_KV_B):
            out_ref[hh] = rope(hh).astype(BF16)
        for hh in range(N_KV_B, 2 * N_KV_B):
            out_ref[hh] = _head(y, hh).astype(BF16)


def _inproj_b(x2, g, w, gain_cols, tables, seq):
    t, d = x2.shape
    bm = _pick(seq, 512)
    nps = seq // bm
    bn = HEADS_PER_STEP * HEAD_DIM
    col0 = N_COLS_A // HEADS_PER_STEP
    tab_spec = pl.BlockSpec((bm, HEAD_DIM), lambda i, j: (i % nps, 0))
    return pl.pallas_call(
        _inproj_b_kernel,
        grid=(t // bm, N_COLS_B // HEADS_PER_STEP),
        in_specs=[
            pl.BlockSpec((bm, d), lambda i, j: (i, 0)),
            pl.BlockSpec((1, d), lambda i, j: (0, 0)),
            pl.BlockSpec((d, bn), lambda i, j: (0, col0 + j)),
            pl.BlockSpec((1, bn), lambda i, j: (0, col0 + j)),
            tab_spec, tab_spec, tab_spec,
        ],
        out_specs=pl.BlockSpec((HEADS_PER_STEP, bm, HEAD_DIM), lambda i, j: (j, i, 0)),
        out_shape=jax.ShapeDtypeStruct((N_COLS_B, t, HEAD_DIM), BF16),
        scratch_shapes=[pltpu.VMEM((bm, d), BF16)],
        compiler_params=_params("parallel", "arbitrary"),
        name="inproj_b",
    )(x2, g, w, gain_cols, *tables)


QB_DIL = 128
KW_DIL = QB_DIL + 2 * BAND_HALF
DIL_UNROLL = 4


def _dilated_kernel(q_ref, k_ref, v_ref, o_ref, lse_ref, *, seg, shifted):
    h = pl.program_id(1)
    rows = q_ref.shape[0]
    rel0 = (lax.broadcasted_iota(jnp.int32, (QB_DIL, KW_DIL), 1)
            - lax.broadcasted_iota(jnp.int32, (QB_DIL, KW_DIL), 0))
    lane = lax.broadcasted_iota(jnp.int32, (QB_DIL, LANES), 1)

    @pl.when(h == 0)
    def _():
        lse_ref[...] = jnp.zeros_like(lse_ref)

    def body(blk, carry):
        l0 = pl.multiple_of(blk * QB_DIL, QB_DIL)
        seg0 = l0 & (-seg)
        w0 = pl.multiple_of(jnp.clip(l0 - BAND_HALF, seg0, seg0 + seg - KW_DIL), BAND_HALF)
        q = q_ref[pl.ds(l0, QB_DIL), :]
        k = k_ref[pl.ds(w0, KW_DIL), :]
        v = v_ref[pl.ds(w0, KW_DIL), :]
        s = lax.dot_general(q, k, (((1,), (1,)), ((), ())), preferred_element_type=F32)
        s = jnp.where(jnp.abs(rel0 + (w0 - l0)) <= BAND_HALF, s, NEG)
        if shifted:
            m = jnp.max(s, axis=-1, keepdims=True)
            p = jnp.exp(s - m)
        else:
            p = jnp.exp(s)
        den = jnp.sum(p, axis=-1, keepdims=True)
        o = jnp.dot(p.astype(BF16), v, preferred_element_type=F32) / den
        lse = jnp.log(den) + m if shifted else jnp.log(den)
        o_ref[pl.ds(l0, QB_DIL), :] = o.astype(BF16)
        lse_ref[pl.ds(l0, QB_DIL), :] = jnp.where(lane == h, lse, lse_ref[pl.ds(l0, QB_DIL), :])
        return carry

    lax.fori_loop(0, rows // QB_DIL, body, 0, unroll=DIL_UNROLL)


def _dilated_branch(qkv_rm, batch, seq, dil, shifted):
    seg = seq // dil
    assert seg >= KW_DIL and seg % QB_DIL == 0 and seg & (seg - 1) == 0

    def spec(off):
        return pl.BlockSpec((None, None, seq, HEAD_DIM), lambda b, h: (off + h, b, 0, 0))

    return pl.pallas_call(
        functools.partial(_dilated_kernel, seg=seg, shifted=shifted),
        grid=(batch, N_HEADS_A),
        in_specs=[spec(0), spec(N_HEADS_A), spec(2 * N_HEADS_A)],
        out_specs=[pl.BlockSpec((None, None, seq, HEAD_DIM), lambda b, h: (h, b, 0, 0)),
                   pl.BlockSpec((None, seq, LANES), lambda b, h: (b, 0, 0))],
        out_shape=[jax.ShapeDtypeStruct((N_HEADS_A, batch, seq, HEAD_DIM), BF16),
                   jax.ShapeDtypeStruct((batch, seq, LANES), F32)],
        compiler_params=_params("parallel", "arbitrary"),
        name=f"dilated_d{dil}" + ("_shifted" if shifted else ""),
    )(qkv_rm, qkv_rm, qkv_rm)


def _gqa_kernel(q_ref, k_ref, v_ref, o_ref, *, bk, shifted):
    rep, bq, dh = q_ref.shape
    seq = k_ref.shape[0]
    rows = rep * bq
    q = q_ref[...].reshape(rows, dh)

    def scores(c):
        c0 = pl.multiple_of(c * bk, bk)
        k = k_ref[pl.ds(c0, bk), :]
        v = v_ref[pl.ds(c0, bk), :]
        return lax.dot_general(q, k, (((1,), (1,)), ((), ())), preferred_element_type=F32), v

    if shifted:
        def body(c, carry):
            m, l, acc = carry
            s, v = scores(c)
            m_new = jnp.maximum(m, jnp.max(s, axis=-1, keepdims=True))
            alpha = jnp.exp(m - m_new)
            p = jnp.exp(s - m_new)
            l = alpha * l + jnp.sum(p, axis=-1, keepdims=True)
            acc = alpha * acc + jnp.dot(p.astype(BF16), v, preferred_element_type=F32)
            return m_new, l, acc

        init = (jnp.full((rows, 1), NEG, F32), jnp.zeros((rows, 1), F32), jnp.zeros((rows, dh), F32))
        _, l, acc = lax.fori_loop(0, seq // bk, body, init)
    else:
        def body(c, carry):
            lsum, acc = carry
            s, v = scores(c)
            p = jnp.exp(s)
            for tile in range(bk // LANES):
                lsum = lsum + p[:, tile * LANES:(tile + 1) * LANES]
            acc = acc + jnp.dot(p.astype(BF16), v, preferred_element_type=F32)
            return lsum, acc

        init = (jnp.zeros((rows, LANES), F32), jnp.zeros((rows, dh), F32))
        lsum, acc = lax.fori_loop(0, seq // bk, body, init, unroll=2)
        l = jnp.sum(lsum, axis=-1, keepdims=True)
    o = acc / l
    for h in range(rep):
        o_ref[:, h * dh:(h + 1) * dh] = o[h * bq:(h + 1) * bq].astype(o_ref.dtype)


def _gqa(proj_b, batch, seq, shifted):
    t = batch * seq
    bq = _pick(seq, 256)
    bk = _pick(seq, 512)
    nq = seq // bq
    view = proj_b.reshape(N_COLS_B, batch, seq, HEAD_DIM)
    return pl.pallas_call(
        functools.partial(_gqa_kernel, bk=bk, shifted=shifted),
        grid=(batch, N_KV_B, nq),
        in_specs=[
            pl.BlockSpec((GQA_REP, None, bq, HEAD_DIM), lambda b, g, i: (g, b, i, 0)),
            pl.BlockSpec((None, None, seq, HEAD_DIM), lambda b, g, i: (N_HEADS_B + g, b, 0, 0)),
            pl.BlockSpec((None, None, seq, HEAD_DIM), lambda b, g, i: (N_HEADS_B + N_KV_B + g, b, 0, 0)),
        ],
        out_specs=pl.BlockSpec((bq, GQA_REP * HEAD_DIM), lambda b, g, i: (b * nq + i, g)),
        out_shape=jax.ShapeDtypeStruct((t, N_HEADS_B * HEAD_DIM), BF16),
        compiler_params=_params("parallel", "parallel", "arbitrary"),
        name="gqa" + ("_shifted" if shifted else ""),
    )(view, view, view)


def _outproj_kernel(o1_ref, o4_ref, o16_ref, l1_ref, l4_ref, l16_ref, ob_ref, ona_ref, onb_ref,
                    w_ref, x_ref, out_ref, mixed_ref, oa_ref, t4_ref, t16_ref):
    j = pl.program_id(1)
    wa = N_HEADS_A * HEAD_DIM
    wb = ob_ref.shape[1]
    bm = oa_ref.shape[0]

    def to_token_order(tmp_ref, load_residue, dil):
        for r in range(dil):
            tmp_ref[pl.ds(r, bm // dil, stride=dil), :] = load_residue(r)
        return tmp_ref[...]

    @pl.when(j == 0)
    def _():
        l1 = l1_ref[...]
        l4 = to_token_order(t4_ref, lambda r: l4_ref[r], 4)
        l16 = to_token_order(t16_ref, lambda r: l16_ref[r], 16)
        mx = jnp.maximum(jnp.maximum(l1, l4), l16)
        e1, e4, e16 = jnp.exp(l1 - mx), jnp.exp(l4 - mx), jnp.exp(l16 - mx)
        den = e1 + e4 + e16
        w1, w4, w16 = e1 / den, e4 / den, e16 / den
        ss = jnp.zeros((bm, HEAD_DIM), F32)
        for h in range(N_HEADS_A):
            o4 = to_token_order(t4_ref, lambda r: o4_ref[h, r].astype(F32), 4)
            o16 = to_token_order(t16_ref, lambda r: o16_ref[h, r].astype(F32), 16)
            oa = (w1[:, h:h + 1] * o1_ref[h].astype(F32) + w4[:, h:h + 1] * o4
                  + w16[:, h:h + 1] * o16)
            oa_ref[:, h * HEAD_DIM:(h + 1) * HEAD_DIM] = oa
            ss = ss + oa * oa
        ra = lax.rsqrt(jnp.sum(ss, axis=-1, keepdims=True) / wa + EPS)
        mixed_ref[:, :wa] = (oa_ref[...] * ra * ona_ref[...]).astype(BF16)
        ob = ob_ref[...].astype(F32)
        rb = lax.rsqrt(jnp.mean(ob * ob, axis=-1, keepdims=True) + EPS)
        mixed_ref[:, wa:wa + wb] = (ob * rb * onb_ref[...]).astype(BF16)

    out_ref[...] = x_ref[...] + jnp.dot(mixed_ref[...], w_ref[...], preferred_element_type=F32)


def _outproj(o_branches, lse_branches, ob, on_a, on_b, w, x2, batch, seq):
    t, d = x2.shape
    bm = _pick(seq, 512)
    nps = seq // bm
    bn = _pick(d, 512)
    wa = N_HEADS_A * HEAD_DIM
    wb = ob.shape[1]
    o1, o4, o16 = o_branches
    l1, l4, l16 = lse_branches
    o1 = o1.reshape(N_HEADS_A, t, HEAD_DIM)
    l1 = l1.reshape(t, LANES)

    def o_spec(dil):
        return pl.BlockSpec((N_HEADS_A, None, dil, bm // dil, HEAD_DIM),
                            lambda i, j: (0, i // nps, 0, i % nps, 0))

    def l_spec(dil):
        return pl.BlockSpec((None, dil, bm // dil, LANES), lambda i, j: (i // nps, 0, i % nps, 0))

    def rm(a, dil):
        return a.reshape(a.shape[:-2] + (dil, seq // dil, a.shape[-1]))

    return pl.pallas_call(
        _outproj_kernel,
        grid=(t // bm, d // bn),
        in_specs=[
            pl.BlockSpec((N_HEADS_A, bm, HEAD_DIM), lambda i, j: (0, i, 0)), o_spec(4), o_spec(16),
            pl.BlockSpec((bm, LANES), lambda i, j: (i, 0)), l_spec(4), l_spec(16),
            pl.BlockSpec((bm, wb), lambda i, j: (i, 0)),
            pl.BlockSpec((1, wa), lambda i, j: (0, 0)),
            pl.BlockSpec((1, wb), lambda i, j: (0, 0)),
            pl.BlockSpec((wa + wb, bn), lambda i, j: (0, j)),
            pl.BlockSpec((bm, bn), lambda i, j: (i, j)),
        ],
        out_specs=pl.BlockSpec((bm, bn), lambda i, j: (i, j)),
        out_shape=jax.ShapeDtypeStruct((t, d), F32),
        scratch_shapes=[pltpu.VMEM((bm, wa + wb), BF16), pltpu.VMEM((bm, wa), F32),
                        pltpu.VMEM((bm, HEAD_DIM), F32), pltpu.VMEM((bm, HEAD_DIM), F32)],
        compiler_params=_params("parallel", "arbitrary"),
        name="outproj",
    )(o1, rm(o4, 4), rm(o16, 16), l1, rm(l4, 4), rm(l16, 16), ob, on_a, on_b, w, x2)


def _swiglu_step(h, wg_ref, wu_ref, wd_ref, acc_ref, j):
    g = jnp.dot(h, wg_ref[...], preferred_element_type=F32)
    u = jnp.dot(h, wu_ref[...], preferred_element_type=F32)
    a = (g * jax.nn.sigmoid(g) * u).astype(BF16)
    part = jnp.dot(a, wd_ref[...], preferred_element_type=F32)

    @pl.when(j == 0)
    def _():
        acc_ref[...] = part

    @pl.when(j > 0)
    def _():
        acc_ref[...] += part


def _dense_ffn_kernel(x_ref, g_ref, wg_ref, wu_ref, wd_ref, out_ref, hn_ref, acc_ref):
    j = pl.program_id(1)

    @pl.when(j == 0)
    def _():
        _rms_to_bf16(x_ref, g_ref, hn_ref)

    _swiglu_step(hn_ref[...], wg_ref, wu_ref, wd_ref, acc_ref, j)

    @pl.when(j == pl.num_programs(1) - 1)
    def _():
        out_ref[...] = x_ref[...] + acc_ref[...]


def _dense_ffn(x2, g, wg, wu, wd):
    t, d = x2.shape
    f = wg.shape[1]
    bm = _pick(t, 512)
    bf = _pick(f, 512)
    return pl.pallas_call(
        _dense_ffn_kernel,
        grid=(t // bm, f // bf),
        in_specs=[
            pl.BlockSpec((bm, d), lambda i, j: (i, 0)),
            pl.BlockSpec((1, d), lambda i, j: (0, 0)),
            pl.BlockSpec((d, bf), lambda i, j: (0, j)),
            pl.BlockSpec((d, bf), lambda i, j: (0, j)),
            pl.BlockSpec((bf, d), lambda i, j: (j, 0)),
        ],
        out_specs=pl.BlockSpec((bm, d), lambda i, j: (i, 0)),
        out_shape=jax.ShapeDtypeStruct((t, d), F32),
        scratch_shapes=[pltpu.VMEM((bm, d), BF16), pltpu.VMEM((bm, d), F32)],
        compiler_params=_params("parallel", "arbitrary"),
        name="dense_ffn",
    )(x2, g, wg, wu, wd)


INFO_E1, INFO_E2, INFO_R1, INFO_R2, INFO_G1, INFO_G2 = range(6)


def _router_kernel(x_ref, g_ref, wr_ref, hn_ref, info_ref, cnt_ref, run_ref):
    i = pl.program_id(0)
    bm = x_ref.shape[0]

    @pl.when(i == 0)
    def _():
        run_ref[...] = jnp.zeros_like(run_ref)

    x = x_ref[...]
    ms = jnp.mean(x * x, axis=-1, keepdims=True)
    hn = x * lax.rsqrt(ms + EPS) * g_ref[...]
    hn_ref[...] = hn
    logits = jnp.dot(hn, wr_ref[...], preferred_element_type=F32, precision=lax.Precision.HIGHEST)
    lane = lax.broadcasted_iota(jnp.int32, (bm, LANES), 1)
    lg = jnp.where(lane < N_EXPERTS, logits, -jnp.inf)
    v1 = jnp.max(lg, axis=-1, keepdims=True)
    i1 = jnp.min(jnp.where(lg == v1, lane, LANES), axis=-1, keepdims=True)
    lg2 = jnp.where(lane == i1, -jnp.inf, lg)
    v2 = jnp.max(lg2, axis=-1, keepdims=True)
    i2 = jnp.min(jnp.where(lg2 == v2, lane, LANES), axis=-1, keepdims=True)
    e = jnp.exp(v2 - v1)
    g1 = 1.0 / (1.0 + e)
    g2 = e / (1.0 + e)
    hot1 = lane == i1
    hot2 = lane == i2
    onehot = (hot1 | hot2).astype(F32)
    tri = (lax.broadcasted_iota(jnp.int32, (bm, bm), 1)
           < lax.broadcasted_iota(jnp.int32, (bm, bm), 0)).astype(BF16)
    rank = jnp.dot(tri, onehot.astype(BF16), preferred_element_type=F32) + run_ref[0:1, :]
    r1 = jnp.sum(jnp.where(hot1, rank, 0.0), axis=-1, keepdims=True)
    r2 = jnp.sum(jnp.where(hot2, rank, 0.0), axis=-1, keepdims=True)
    run_ref[...] = run_ref[...] + jnp.sum(onehot, axis=0, keepdims=True)
    info = jnp.zeros((bm, LANES), F32)
    for col, val in ((INFO_E1, i1.astype(F32)), (INFO_E2, i2.astype(F32)), (INFO_R1, r1),
                     (INFO_R2, r2), (INFO_G1, g1), (INFO_G2, g2)):
        info = jnp.where(lane == col, val, info)
    info_ref[...] = info
    cnt_ref[...] = run_ref[...]


def _router(x2, g, wr_pad):
    t, d = x2.shape
    bm = _pick(t, 512)
    return pl.pallas_call(
        _router_kernel,
        grid=(t // bm,),
        in_specs=[
            pl.BlockSpec((bm, d), lambda i: (i, 0)),
            pl.BlockSpec((1, d), lambda i: (0, 0)),
            pl.BlockSpec((d, LANES), lambda i: (0, 0)),
        ],
        out_specs=[
            pl.BlockSpec((bm, d), lambda i: (i, 0)),
            pl.BlockSpec((bm, LANES), lambda i: (i, 0)),
            pl.BlockSpec((8, LANES), lambda i: (0, 0)),
        ],
        out_shape=[jax.ShapeDtypeStruct((t, d), F32), jax.ShapeDtypeStruct((t, LANES), F32),
                   jax.ShapeDtypeStruct((8, LANES), F32)],
        scratch_shapes=[pltpu.VMEM((8, LANES), F32)],
        compiler_params=_params("arbitrary"),
        name="router",
    )(x2, g, wr_pad)


def _scatter_kernel(pos1_ref, pos2_ref, hn_ref, hs_in_ref, hs_ref, sem):
    del hs_in_ref
    bt = hn_ref.shape[0]

    def copies(k):
        src = hn_ref.at[pl.ds(k, 1), :]
        return (pltpu.make_async_copy(src, hs_ref.at[pl.ds(pos1_ref[0, 0, k], 1), :], sem.at[0]),
                pltpu.make_async_copy(src, hs_ref.at[pl.ds(pos2_ref[0, 0, k], 1), :], sem.at[1]))

    def start(k, c):
        a, b = copies(k)
        a.start()
        b.start()
        return c

    def wait(k, c):
        a, b = copies(k)
        a.wait()
        b.wait()
        return c

    lax.fori_loop(0, bt, start, 0)
    lax.fori_loop(0, bt, wait, 0)


def _scatter_rows(hn, pos1, pos2, n_rows):
    t, d = hn.shape
    bt = _pick(t, 512)
    smem_spec = pl.BlockSpec((1, 1, bt), lambda i: (i, 0, 0), memory_space=pltpu.SMEM)
    any_spec = pl.BlockSpec(memory_space=pl.ANY)
    return pl.pallas_call(
        _scatter_kernel,
        grid=(t // bt,),
        in_specs=[smem_spec, smem_spec, pl.BlockSpec((bt, d), lambda i: (i, 0)), any_spec],
        out_specs=any_spec,
        out_shape=jax.ShapeDtypeStruct((n_rows, d), hn.dtype),
        scratch_shapes=[pltpu.SemaphoreType.DMA((2,))],
        input_output_aliases={3: 0},
        compiler_params=_params("arbitrary"),
        name="scatter_rows",
    )(pos1.reshape(t // bt, 1, bt), pos2.reshape(t // bt, 1, bt), hn, jnp.zeros((n_rows, d), hn.dtype))


def _grouped_ffn_kernel(be_ref, na_ref, hs_ref, wg_ref, wu_ref, wd_ref, out_ref, hb_ref, acc_ref):
    del be_ref
    i = pl.program_id(0)
    j = pl.program_id(1)
    active = i < na_ref[0]

    @pl.when(active & (j == 0))
    def _():
        hb_ref[...] = hs_ref[...].astype(BF16)

    @pl.when(active)
    def _():
        _swiglu_step(hb_ref[...], wg_ref, wu_ref, wd_ref, acc_ref, j)

    @pl.when(j == pl.num_programs(1) - 1)
    def _():
        @pl.when(active)
        def _():
            out_ref[...] = acc_ref[...]

        @pl.when(jnp.logical_not(active))
        def _():
            out_ref[...] = jnp.zeros_like(out_ref)


def _grouped_ffn(hs, block_expert, n_active, wg, wu, wd, bm):
    p, d = hs.shape
    f = wg.shape[2]
    bf = _pick(f, 512)
    nf = f // bf

    def jj(i, j, na):
        return jnp.where(i < na[0], j, nf - 1)

    grid_spec = pltpu.PrefetchScalarGridSpec(
        num_scalar_prefetch=2,
        grid=(p // bm, nf),
        in_specs=[
            pl.BlockSpec((bm, d), lambda i, j, be, na: (jnp.minimum(i, na[0] - 1), 0)),
            pl.BlockSpec((None, d, bf), lambda i, j, be, na: (be[i], 0, jj(i, j, na))),
            pl.BlockSpec((None, d, bf), lambda i, j, be, na: (be[i], 0, jj(i, j, na))),
            pl.BlockSpec((None, bf, d), lambda i, j, be, na: (be[i], jj(i, j, na), 0)),
        ],
        out_specs=pl.BlockSpec((bm, d), lambda i, j, be, na: (i, 0)),
        scratch_shapes=[pltpu.VMEM((bm, d), BF16), pltpu.VMEM((bm, d), F32)],
    )
    return pl.pallas_call(
        _grouped_ffn_kernel,
        grid_spec=grid_spec,
        out_shape=jax.ShapeDtypeStruct((p, d), F32),
        compiler_params=_params("arbitrary", "arbitrary"),
        name="grouped_ffn",
    )(block_expert, n_active, hs, wg, wu, wd)


def _combine_kernel(pos1_ref, pos2_ref, x_ref, info_ref, y_ref, out_ref, buf1_ref, buf2_ref, sem):
    bt = x_ref.shape[0]

    def copies(k):
        return (pltpu.make_async_copy(y_ref.at[pl.ds(pos1_ref[0, 0, k], 1), :],
                                      buf1_ref.at[pl.ds(k, 1), :], sem.at[0]),
                pltpu.make_async_copy(y_ref.at[pl.ds(pos2_ref[0, 0, k], 1), :],
                                      buf2_ref.at[pl.ds(k, 1), :], sem.at[1]))

    def start(k, c):
        a, b = copies(k)
        a.start()
        b.start()
        return c

    def wait(k, c):
        a, b = copies(k)
        a.wait()
        b.wait()
        return c

    lax.fori_loop(0, bt, start, 0)
    lax.fori_loop(0, bt, wait, 0)
    info = info_ref[...]
    g1 = info[:, INFO_G1:INFO_G1 + 1]
    g2 = info[:, INFO_G2:INFO_G2 + 1]
    out_ref[...] = x_ref[...] + g1 * buf1_ref[...] + g2 * buf2_ref[...]


def _combine(x2, info, y, pos1, pos2):
    t, d = x2.shape
    bt = _pick(t, 256)
    smem_spec = pl.BlockSpec((1, 1, bt), lambda i: (i, 0, 0), memory_space=pltpu.SMEM)
    return pl.pallas_call(
        _combine_kernel,
        grid=(t // bt,),
        in_specs=[
            smem_spec, smem_spec,
            pl.BlockSpec((bt, d), lambda i: (i, 0)),
            pl.BlockSpec((bt, LANES), lambda i: (i, 0)),
            pl.BlockSpec(memory_space=pl.ANY),
        ],
        out_specs=pl.BlockSpec((bt, d), lambda i: (i, 0)),
        out_shape=jax.ShapeDtypeStruct((t, d), F32),
        scratch_shapes=[pltpu.VMEM((bt, d), F32), pltpu.VMEM((bt, d), F32),
                        pltpu.SemaphoreType.DMA((2,))],
        compiler_params=_params("arbitrary"),
        name="combine",
    )(pos1.reshape(t // bt, 1, bt), pos2.reshape(t // bt, 1, bt), x2, info, y)


MOE_BLOCK_ROWS = 512


def _moe_ffn(x2, g, wr, wg, wu, wd):
    t, d = x2.shape
    bm = MOE_BLOCK_ROWS
    wr_pad = jnp.zeros((d, LANES), F32).at[:, :N_EXPERTS].set(wr)
    hn, info, cnt = _router(x2, g, wr_pad)
    e1 = info[:, INFO_E1].astype(jnp.int32)
    e2 = info[:, INFO_E2].astype(jnp.int32)
    r1 = info[:, INFO_R1].astype(jnp.int32)
    r2 = info[:, INFO_R2].astype(jnp.int32)
    counts = cnt[0, :N_EXPERTS].astype(jnp.int32)
    blocks_per_expert = (counts + bm - 1) // bm
    block_end = jnp.cumsum(blocks_per_expert)
    offsets = (block_end - blocks_per_expert) * bm
    pos1 = offsets[e1] + r1
    pos2 = offsets[e2] + r2
    n_blocks = (2 * t) // bm + N_EXPERTS
    n_active = block_end[-1:]
    ids = jnp.minimum(jnp.arange(n_blocks, dtype=jnp.int32), n_active[0] - 1)
    block_expert = jnp.sum((ids[:, None] >= block_end[None, :]).astype(jnp.int32), axis=1)
    block_expert = jnp.minimum(block_expert, N_EXPERTS - 1)
    hs = _scatter_rows(hn, pos1, pos2, n_blocks * bm)
    y = _grouped_ffn(hs, block_expert, n_active.astype(jnp.int32), wg, wu, wd, bm)
    return _combine(x2, info, y, pos1, pos2)


def _attention(proj_a, proj_b, batch, seq, shifted):
    views = [a.reshape(N_COLS_A, batch, seq, HEAD_DIM) for a in proj_a]
    branches = [_dilated_branch(v, batch, seq, dil, shifted) for v, dil in zip(views, DILATIONS)]
    ob = _gqa(proj_b, batch, seq, shifted)
    return [b[0] for b in branches], [b[1] for b in branches], ob


def kernel(x, g_mix, w_in, qn_a, kn_a, qn_b, kn_b, on_a, on_b, w_out, g_ffn, w_gate, w_up, w_down,
           w_router, we_gate, we_up, we_down):
    batch, seq, d = x.shape
    depth = w_in.shape[0]
    t = batch * seq
    assert all(w // (2 * dil) == BAND_HALF for w, dil in DILATED_PATTERNS) and DILATIONS == (1, 4, 16)
    assert w_in.shape[2] == (N_COLS_A + N_COLS_B) * HEAD_DIM
    scale = HEAD_DIM ** -0.5
    tables = _rope_tables(seq)
    x2 = x.reshape(t, d)
    for l in range(depth):
        gains_a = jnp.concatenate([jnp.tile(qn_a[l] * scale, N_HEADS_A), jnp.tile(kn_a[l], N_HEADS_A),
                                   jnp.ones((N_HEADS_A * HEAD_DIM,), F32)])[None, :]
        gains_b = jnp.concatenate([jnp.zeros((N_COLS_A * HEAD_DIM,), F32),
                                   jnp.tile(qn_b[l] * scale, N_HEADS_B), jnp.tile(kn_b[l], N_KV_B),
                                   jnp.ones((N_KV_B * HEAD_DIM,), F32)])[None, :]
        bound = 1.02 * HEAD_DIM ** 0.5 * jnp.maximum(
            jnp.max(jnp.abs(qn_a[l])) * jnp.max(jnp.abs(kn_a[l])),
            jnp.max(jnp.abs(qn_b[l])) * jnp.max(jnp.abs(kn_b[l])))
        w_in_l = w_in[l].astype(BF16)
        g_l = g_mix[l][None, :]
        proj_a = _inproj_a(x2, g_l, w_in_l, gains_a, tables[0], batch, seq)
        proj_b = _inproj_b(x2, g_l, w_in_l, gains_b, tables[1], seq)
        attend = functools.partial(_attention, proj_a, proj_b, batch, seq)
        o_br, lse_br, ob = lax.cond(bound <= MAX_UNSHIFTED_SCORE, lambda: attend(False),
                                    lambda: attend(True))
        x2 = _outproj(o_br, lse_br, ob, on_a[l][None, :], on_b[l][None, :], w_out[l].astype(BF16),
                      x2, batch, seq)
        i = l // 2
        if l % 2 == 0:
            x2 = _dense_ffn(x2, g_ffn[l][None, :], w_gate[i].astype(BF16), w_up[i].astype(BF16),
                            w_down[i].astype(BF16))
        else:
            x2 = _moe_ffn(x2, g_ffn[l][None, :], w_router[i], we_gate[i].astype(BF16),
                          we_up[i].astype(BF16), we_down[i].astype(BF16))
    return x2.reshape(batch, seq, d)
```

```python
import functools

import jax
import jax.numpy as jnp
from jax import lax
from jax.experimental import pallas as pl
from jax.experimental.pallas import tpu as pltpu

HEAD_DIM = 128
N_HEADS_A = 8
N_HEADS_B = 8
N_KV_B = 2
GQA_REP = N_HEADS_B // N_KV_B
N_COLS_A = 3 * N_HEADS_A
N_COLS_B = N_HEADS_B + 2 * N_KV_B
DILATED_PATTERNS = ((128, 1), (512, 4), (2048, 16))
DILATIONS = tuple(d for _, d in DILATED_PATTERNS)
BAND_HALF = 64
GRID_W = 64
ROPE_THETA = 10000.0
N_EXPERTS = 8
EPS = 1e-6
NEG = -1e30
MAX_UNSHIFTED_SCORE = 40.0

HEADS_PER_STEP = 4
LANES = 128
DMA_UNROLL = 8
V7X_VMEM_LIMIT_BYTES = 56 * 1024 * 1024

BF16 = jnp.bfloat16
F32 = jnp.float32


def _params(*semantics):
    return pltpu.CompilerParams(dimension_semantics=semantics, vmem_limit_bytes=V7X_VMEM_LIMIT_BYTES)


def _pick(n, pref):
    b = min(n, pref)
    assert n % b == 0, (n, pref)
    return b


def _rope_angles(pos, dim):
    inv = ROPE_THETA ** (-jnp.arange(0, dim, 2, dtype=F32) / dim)
    ang = pos.astype(F32)[:, None] * inv[None, :]
    return jnp.cos(ang), jnp.sin(ang)


def _rope_tables(seq):
    pos = jnp.arange(seq)
    c1, s1 = _rope_angles(pos, HEAD_DIM)
    cos_a = jnp.concatenate([c1, c1], axis=-1)
    sin_a = jnp.concatenate([-s1, s1], axis=-1)
    cr, sr = _rope_angles(pos // GRID_W, HEAD_DIM // 2)
    cc, sc = _rope_angles(pos % GRID_W, HEAD_DIM // 2)
    z = jnp.zeros_like(sr)
    cos_b = jnp.concatenate([cr, cr, cc, cc], axis=-1)
    sin_b_up = jnp.concatenate([-sr, z, -sc, z], axis=-1)
    sin_b_dn = jnp.concatenate([z, sr, z, sc], axis=-1)
    return (cos_a, sin_a), (cos_b, sin_b_up, sin_b_dn)


def _rms_to_bf16(x_ref, g_ref, hn_ref):
    x = x_ref[...]
    ms = jnp.mean(x * x, axis=-1, keepdims=True)
    hn_ref[...] = (x * lax.rsqrt(ms + EPS) * g_ref[...]).astype(BF16)


def _head(y, hh):
    return y[:, hh * HEAD_DIM:(hh + 1) * HEAD_DIM]


def _inproj_kernel(x_ref, g_ref, w_ref, gain_ref, cosa_ref, sina_ref, cosb_ref, sinbu_ref, sinbd_ref,
                   a1_ref, a4_ref, a16_ref, b_ref, hn_ref, tmp_ref):
    bm = x_ref.shape[0]
    _rms_to_bf16(x_ref, g_ref, hn_ref)
    width = HEADS_PER_STEP * HEAD_DIM
    for c in range((N_COLS_A + N_COLS_B) // HEADS_PER_STEP):
        y = jnp.dot(hn_ref[...], w_ref[:, c * width:(c + 1) * width], preferred_element_type=F32)
        for hh in range(HEADS_PER_STEP):
            col = c * HEADS_PER_STEP + hh
            gain = gain_ref[:, col * HEAD_DIM:(col + 1) * HEAD_DIM]
            yh = _head(y, hh)
            is_a = col < N_COLS_A
            is_v = (2 * N_HEADS_A <= col < N_COLS_A) or col >= N_COLS_A + N_HEADS_B + N_KV_B
            if is_v:
                res = yh
            else:
                ms = jnp.mean(yh * yh, axis=-1, keepdims=True)
                yn = yh * lax.rsqrt(ms + EPS) * gain
                if is_a:
                    res = yn * cosa_ref[...] + pltpu.roll(yn, 64, 1) * sina_ref[...]
                else:
                    res = (yn * cosb_ref[...] + pltpu.roll(yn, 96, 1) * sinbu_ref[...]
                           + pltpu.roll(yn, 32, 1) * sinbd_ref[...])
            if is_a:
                a1_ref[col] = res.astype(BF16)
                tmp_ref[col] = res
                for out_ref, dil in ((a4_ref, 4), (a16_ref, 16)):
                    for r in range(dil):
                        out_ref[col, r] = tmp_ref[col, pl.ds(r, bm // dil, stride=dil), :].astype(BF16)
            else:
                b_ref[col - N_COLS_A] = res.astype(BF16)


def _inproj(x2, g, w, gain_cols, tables, batch, seq):
    t, d = x2.shape
    bm = _pick(seq, 256)
    nps = seq // bm
    n = w.shape[1]
    tab_spec = pl.BlockSpec((bm, HEAD_DIM), lambda i: (i % nps, 0))

    def rm_spec(dil):
        return pl.BlockSpec((N_COLS_A, None, dil, bm // dil, HEAD_DIM),
                            lambda i: (0, i // nps, 0, i % nps, 0))

    def rm_shape(dil):
        return jax.ShapeDtypeStruct((N_COLS_A, batch, dil, seq // dil, HEAD_DIM), BF16)

    return pl.pallas_call(
        _inproj_kernel,
        grid=(t // bm,),
        in_specs=[
            pl.BlockSpec((bm, d), lambda i: (i, 0)),
            pl.BlockSpec((1, d), lambda i: (0, 0)),
            pl.BlockSpec((d, n), lambda i: (0, 0), pipeline_mode=pl.Buffered(1)),
            pl.BlockSpec((1, n), lambda i: (0, 0)),
            tab_spec, tab_spec, tab_spec, tab_spec, tab_spec,
        ],
        out_specs=[pl.BlockSpec((N_COLS_A, bm, HEAD_DIM), lambda i: (0, i, 0)), rm_spec(4), rm_spec(16),
                   pl.BlockSpec((N_COLS_B, bm, HEAD_DIM), lambda i: (0, i, 0))],
        out_shape=[jax.ShapeDtypeStruct((N_COLS_A, t, HEAD_DIM), BF16), rm_shape(4), rm_shape(16),
                   jax.ShapeDtypeStruct((N_COLS_B, t, HEAD_DIM), BF16)],
        scratch_shapes=[pltpu.VMEM((bm, d), BF16), pltpu.VMEM((N_COLS_A, bm, HEAD_DIM), F32)],
        compiler_params=_params("parallel"),
        name="inproj",
    )(x2, g, w, gain_cols, *tables)


QB_DIL = 128
KW_DIL = QB_DIL + 2 * BAND_HALF
DIL_UNROLL = 8


def _dilated_kernel(q_ref, k_ref, v_ref, o_ref, lse_ref, *, seg, shifted):
    h = pl.program_id(1)
    rows = q_ref.shape[0]
    rel0 = (lax.broadcasted_iota(jnp.int32, (QB_DIL, KW_DIL), 1)
            - lax.broadcasted_iota(jnp.int32, (QB_DIL, KW_DIL), 0))
    lane = lax.broadcasted_iota(jnp.int32, (QB_DIL, LANES), 1)

    @pl.when(h == 0)
    def _():
        lse_ref[...] = jnp.zeros_like(lse_ref)

    def body(blk, carry):
        l0 = pl.multiple_of(blk * QB_DIL, QB_DIL)
        seg0 = l0 & (-seg)
        w0 = pl.multiple_of(jnp.clip(l0 - BAND_HALF, seg0, seg0 + seg - KW_DIL), BAND_HALF)
        q = q_ref[pl.ds(l0, QB_DIL), :]
        k = k_ref[pl.ds(w0, KW_DIL), :]
        v = v_ref[pl.ds(w0, KW_DIL), :]
        s = lax.dot_general(q, k, (((1,), (1,)), ((), ())), preferred_element_type=F32)
        s = jnp.where(jnp.abs(rel0 + (w0 - l0)) <= BAND_HALF, s, NEG)
        if shifted:
            m = jnp.max(s, axis=-1, keepdims=True)
            p = jnp.exp(s - m)
        else:
            p = jnp.exp(s)
        den = jnp.sum(p, axis=-1, keepdims=True)
        o = jnp.dot(p.astype(BF16), v, preferred_element_type=F32) / den
        lse = jnp.log(den) + m if shifted else jnp.log(den)
        o_ref[pl.ds(l0, QB_DIL), :] = o.astype(BF16)
        lse_ref[pl.ds(l0, QB_DIL), :] = jnp.where(lane == h, lse, lse_ref[pl.ds(l0, QB_DIL), :])
        return carry

    lax.fori_loop(0, rows // QB_DIL, body, 0, unroll=DIL_UNROLL)


def _dilated_branch(qkv_rm, batch, seq, dil, shifted):
    seg = seq // dil
    assert seg >= KW_DIL and seg % QB_DIL == 0 and seg & (seg - 1) == 0

    def spec(off):
        return pl.BlockSpec((None, None, seq, HEAD_DIM), lambda b, h: (off + h, b, 0, 0))

    return pl.pallas_call(
        functools.partial(_dilated_kernel, seg=seg, shifted=shifted),
        grid=(batch, N_HEADS_A),
        in_specs=[spec(0), spec(N_HEADS_A), spec(2 * N_HEADS_A)],
        out_specs=[pl.BlockSpec((None, None, seq, HEAD_DIM), lambda b, h: (h, b, 0, 0)),
                   pl.BlockSpec((None, seq, LANES), lambda b, h: (b, 0, 0))],
        out_shape=[jax.ShapeDtypeStruct((N_HEADS_A, batch, seq, HEAD_DIM), BF16),
                   jax.ShapeDtypeStruct((batch, seq, LANES), F32)],
        compiler_params=_params("parallel", "arbitrary"),
        name=f"dilated_d{dil}" + ("_shifted" if shifted else ""),
    )(qkv_rm, qkv_rm, qkv_rm)


def _gqa_kernel(q_ref, k_ref, v_ref, o_ref, *, bk, shifted):
    rep, bq, dh = q_ref.shape
    seq = k_ref.shape[0]
    rows = rep * bq
    q = q_ref[...].reshape(rows, dh)

    def scores(c):
        c0 = pl.multiple_of(c * bk, bk)
        k = k_ref[pl.ds(c0, bk), :]
        v = v_ref[pl.ds(c0, bk), :]
        return lax.dot_general(q, k, (((1,), (1,)), ((), ())), preferred_element_type=F32), v

    if shifted:
        def body(c, carry):
            m, l, acc = carry
            s, v = scores(c)
            m_new = jnp.maximum(m, jnp.max(s, axis=-1, keepdims=True))
            alpha = jnp.exp(m - m_new)
            p = jnp.exp(s - m_new)
            l = alpha * l + jnp.sum(p, axis=-1, keepdims=True)
            acc = alpha * acc + jnp.dot(p.astype(BF16), v, preferred_element_type=F32)
            return m_new, l, acc

        init = (jnp.full((rows, 1), NEG, F32), jnp.zeros((rows, 1), F32), jnp.zeros((rows, dh), F32))
        _, l, acc = lax.fori_loop(0, seq // bk, body, init)
    else:
        def body(c, carry):
            lsum, acc = carry
            s, v = scores(c)
            p = jnp.exp(s)
            for tile in range(bk // LANES):
                lsum = lsum + p[:, tile * LANES:(tile + 1) * LANES]
            acc = acc + jnp.dot(p.astype(BF16), v, preferred_element_type=F32)
            return lsum, acc

        init = (jnp.zeros((rows, LANES), F32), jnp.zeros((rows, dh), F32))
        lsum, acc = lax.fori_loop(0, seq // bk, body, init, unroll=2)
        l = jnp.sum(lsum, axis=-1, keepdims=True)
    o = acc / l
    for h in range(rep):
        o_ref[:, h * dh:(h + 1) * dh] = o[h * bq:(h + 1) * bq].astype(o_ref.dtype)


def _gqa(proj_b, batch, seq, shifted):
    t = batch * seq
    bq = _pick(seq, 256)
    bk = _pick(seq, 512)
    nq = seq // bq
    view = proj_b.reshape(N_COLS_B, batch, seq, HEAD_DIM)
    return pl.pallas_call(
        functools.partial(_gqa_kernel, bk=bk, shifted=shifted),
        grid=(batch, N_KV_B, nq),
        in_specs=[
            pl.BlockSpec((GQA_REP, None, bq, HEAD_DIM), lambda b, g, i: (g, b, i, 0)),
            pl.BlockSpec((None, None, seq, HEAD_DIM), lambda b, g, i: (N_HEADS_B + g, b, 0, 0)),
            pl.BlockSpec((None, None, seq, HEAD_DIM), lambda b, g, i: (N_HEADS_B + N_KV_B + g, b, 0, 0)),
        ],
        out_specs=pl.BlockSpec((bq, GQA_REP * HEAD_DIM), lambda b, g, i: (b * nq + i, g)),
        out_shape=jax.ShapeDtypeStruct((t, N_HEADS_B * HEAD_DIM), BF16),
        compiler_params=_params("parallel", "parallel", "arbitrary"),
        name="gqa" + ("_shifted" if shifted else ""),
    )(view, view, view)


def _outproj_kernel(o1_ref, o4_ref, o16_ref, l1_ref, l4_ref, l16_ref, ob_ref, ona_ref, onb_ref,
                    w_ref, x_ref, out_ref, mixed_ref, oa_ref, t4_ref, t16_ref):
    wa = N_HEADS_A * HEAD_DIM
    wb = ob_ref.shape[1]
    bm = oa_ref.shape[0]

    def to_token_order(tmp_ref, slot, load_residue, dil):
        for r in range(dil):
            tmp_ref[slot, pl.ds(r, bm // dil, stride=dil), :] = load_residue(r)
        return tmp_ref[slot]

    l1 = l1_ref[...]
    l4 = to_token_order(t4_ref, N_HEADS_A, lambda r: l4_ref[r], 4)
    l16 = to_token_order(t16_ref, N_HEADS_A, lambda r: l16_ref[r], 16)
    mx = jnp.maximum(jnp.maximum(l1, l4), l16)
    e1, e4, e16 = jnp.exp(l1 - mx), jnp.exp(l4 - mx), jnp.exp(l16 - mx)
    den = e1 + e4 + e16
    w1, w4, w16 = e1 / den, e4 / den, e16 / den
    ss = jnp.zeros((bm, HEAD_DIM), F32)
    for h in range(N_HEADS_A):
        o4 = to_token_order(t4_ref, h, lambda r: o4_ref[h, r].astype(F32), 4)
        o16 = to_token_order(t16_ref, h, lambda r: o16_ref[h, r].astype(F32), 16)
        oa = (w1[:, h:h + 1] * o1_ref[h].astype(F32) + w4[:, h:h + 1] * o4
              + w16[:, h:h + 1] * o16)
        oa_ref[:, h * HEAD_DIM:(h + 1) * HEAD_DIM] = oa
        ss = ss + oa * oa
    ra = lax.rsqrt(jnp.sum(ss, axis=-1, keepdims=True) / wa + EPS)
    mixed_ref[:, :wa] = (oa_ref[...] * ra * ona_ref[...]).astype(BF16)
    ob = ob_ref[...].astype(F32)
    rb = lax.rsqrt(jnp.mean(ob * ob, axis=-1, keepdims=True) + EPS)
    mixed_ref[:, wa:wa + wb] = (ob * rb * onb_ref[...]).astype(BF16)
    out_ref[...] = x_ref[...] + jnp.dot(mixed_ref[...], w_ref[...], preferred_element_type=F32)


def _outproj(o_branches, lse_branches, ob, on_a, on_b, w, x2, batch, seq):
    t, d = x2.shape
    bm = _pick(seq, 256)
    nps = seq // bm
    wa = N_HEADS_A * HEAD_DIM
    wb = ob.shape[1]
    o1, o4, o16 = o_branches
    l1, l4, l16 = lse_branches
    o1 = o1.reshape(N_HEADS_A, t, HEAD_DIM)
    l1 = l1.reshape(t, LANES)

    def o_spec(dil):
        return pl.BlockSpec((N_HEADS_A, None, dil, bm // dil, HEAD_DIM),
                            lambda i: (0, i // nps, 0, i % nps, 0))

    def l_spec(dil):
        return pl.BlockSpec((None, dil, bm // dil, LANES), lambda i: (i // nps, 0, i % nps, 0))

    def rm(a, dil):
        return a.reshape(a.shape[:-2] + (dil, seq // dil, a.shape[-1]))

    return pl.pallas_call(
        _outproj_kernel,
        grid=(t // bm,),
        in_specs=[
            pl.BlockSpec((N_HEADS_A, bm, HEAD_DIM), lambda i: (0, i, 0)), o_spec(4), o_spec(16),
            pl.BlockSpec((bm, LANES), lambda i: (i, 0)), l_spec(4), l_spec(16),
            pl.BlockSpec((bm, wb), lambda i: (i, 0)),
            pl.BlockSpec((1, wa), lambda i: (0, 0)),
            pl.BlockSpec((1, wb), lambda i: (0, 0)),
            pl.BlockSpec((wa + wb, d), lambda i: (0, 0), pipeline_mode=pl.Buffered(1)),
            pl.BlockSpec((bm, d), lambda i: (i, 0)),
        ],
        out_specs=pl.BlockSpec((bm, d), lambda i: (i, 0)),
        out_shape=jax.ShapeDtypeStruct((t, d), F32),
        scratch_shapes=[pltpu.VMEM((bm, wa + wb), BF16), pltpu.VMEM((bm, wa), F32),
                        pltpu.VMEM((N_HEADS_A + 1, bm, HEAD_DIM), F32),
                        pltpu.VMEM((N_HEADS_A + 1, bm, HEAD_DIM), F32)],
        compiler_params=_params("parallel"),
        name="outproj",
    )(o1, rm(o4, 4), rm(o16, 16), l1, rm(l4, 4), rm(l16, 16), ob, on_a, on_b, w, x2)


def _swiglu_step(h, wg_ref, wu_ref, wd_ref, acc_ref):
    g = jnp.dot(h, wg_ref[...], preferred_element_type=F32)
    u = jnp.dot(h, wu_ref[...], preferred_element_type=F32)
    a = (g * jax.nn.sigmoid(g) * u).astype(BF16)
    acc_ref[...] += jnp.dot(a, wd_ref[...], preferred_element_type=F32)


def _dense_ffn_kernel(x_ref, g_ref, wg_ref, wu_ref, wd_ref, out_ref, hn_ref, acc_ref):
    j = pl.program_id(1)

    @pl.when(j == 0)
    def _():
        _rms_to_bf16(x_ref, g_ref, hn_ref)
        acc_ref[...] = jnp.zeros_like(acc_ref)

    _swiglu_step(hn_ref[...], wg_ref, wu_ref, wd_ref, acc_ref)

    @pl.when(j == pl.num_programs(1) - 1)
    def _():
        out_ref[...] = x_ref[...] + acc_ref[...]


def _dense_ffn(x2, g, wg, wu, wd):
    t, d = x2.shape
    f = wg.shape[1]
    bm = _pick(t, 512)
    bf = _pick(f, 512)
    return pl.pallas_call(
        _dense_ffn_kernel,
        grid=(t // bm, f // bf),
        in_specs=[
            pl.BlockSpec((bm, d), lambda i, j: (i, 0)),
            pl.BlockSpec((1, d), lambda i, j: (0, 0)),
            pl.BlockSpec((d, bf), lambda i, j: (0, j)),
            pl.BlockSpec((d, bf), lambda i, j: (0, j)),
            pl.BlockSpec((bf, d), lambda i, j: (j, 0)),
        ],
        out_specs=pl.BlockSpec((bm, d), lambda i, j: (i, 0)),
        out_shape=jax.ShapeDtypeStruct((t, d), F32),
        scratch_shapes=[pltpu.VMEM((bm, d), BF16), pltpu.VMEM((bm, d), F32)],
        compiler_params=_params("parallel", "arbitrary"),
        name="dense_ffn",
    )(x2, g, wg, wu, wd)


INFO_E1, INFO_E2, INFO_R1, INFO_R2, INFO_G1, INFO_G2 = range(6)


def _router_kernel(x_ref, g_ref, wr_ref, hn_ref, info_ref, cnt_ref, run_ref):
    i = pl.program_id(0)
    bm = x_ref.shape[0]

    @pl.when(i == 0)
    def _():
        run_ref[...] = jnp.zeros_like(run_ref)

    x = x_ref[...]
    ms = jnp.mean(x * x, axis=-1, keepdims=True)
    hn = x * lax.rsqrt(ms + EPS) * g_ref[...]
    hn_ref[...] = hn
    logits = jnp.dot(hn, wr_ref[...], preferred_element_type=F32, precision=lax.Precision.HIGHEST)
    lane = lax.broadcasted_iota(jnp.int32, (bm, LANES), 1)
    lg = jnp.where(lane < N_EXPERTS, logits, -jnp.inf)
    v1 = jnp.max(lg, axis=-1, keepdims=True)
    i1 = jnp.min(jnp.where(lg == v1, lane, LANES), axis=-1, keepdims=True)
    lg2 = jnp.where(lane == i1, -jnp.inf, lg)
    v2 = jnp.max(lg2, axis=-1, keepdims=True)
    i2 = jnp.min(jnp.where(lg2 == v2, lane, LANES), axis=-1, keepdims=True)
    e = jnp.exp(v2 - v1)
    g1 = 1.0 / (1.0 + e)
    g2 = e / (1.0 + e)
    hot1 = lane == i1
    hot2 = lane == i2
    onehot = (hot1 | hot2).astype(F32)
    tri = (lax.broadcasted_iota(jnp.int32, (bm, bm), 1)
           < lax.broadcasted_iota(jnp.int32, (bm, bm), 0)).astype(BF16)
    rank = jnp.dot(tri, onehot.astype(BF16), preferred_element_type=F32) + run_ref[0:1, :]
    r1 = jnp.sum(jnp.where(hot1, rank, 0.0), axis=-1, keepdims=True)
    r2 = jnp.sum(jnp.where(hot2, rank, 0.0), axis=-1, keepdims=True)
    run_ref[...] = run_ref[...] + jnp.sum(onehot, axis=0, keepdims=True)
    info = jnp.zeros((bm, LANES), F32)
    for col, val in ((INFO_E1, i1.astype(F32)), (INFO_E2, i2.astype(F32)), (INFO_R1, r1),
                     (INFO_R2, r2), (INFO_G1, g1), (INFO_G2, g2)):
        info = jnp.where(lane == col, val, info)
    info_ref[...] = info
    cnt_ref[...] = run_ref[...]


def _router(x2, g, wr_pad):
    t, d = x2.shape
    bm = _pick(t, 512)
    return pl.pallas_call(
        _router_kernel,
        grid=(t // bm,),
        in_specs=[
            pl.BlockSpec((bm, d), lambda i: (i, 0)),
            pl.BlockSpec((1, d), lambda i: (0, 0)),
            pl.BlockSpec((d, LANES), lambda i: (0, 0)),
        ],
        out_specs=[
            pl.BlockSpec((bm, d), lambda i: (i, 0)),
            pl.BlockSpec((bm, LANES), lambda i: (i, 0)),
            pl.BlockSpec((8, LANES), lambda i: (0, 0)),
        ],
        out_shape=[jax.ShapeDtypeStruct((t, d), F32), jax.ShapeDtypeStruct((t, LANES), F32),
                   jax.ShapeDtypeStruct((8, LANES), F32)],
        scratch_shapes=[pltpu.VMEM((8, LANES), F32)],
        compiler_params=_params("arbitrary"),
        name="router",
    )(x2, g, wr_pad)


def _scatter_kernel(pos1_ref, pos2_ref, hn_ref, hs_in_ref, hs_ref, sem):
    del hs_in_ref
    bt = hn_ref.shape[0]

    def copies(k):
        src = hn_ref.at[pl.ds(k, 1), :]
        return (pltpu.make_async_copy(src, hs_ref.at[pl.ds(pos1_ref[0, 0, k], 1), :], sem.at[0]),
                pltpu.make_async_copy(src, hs_ref.at[pl.ds(pos2_ref[0, 0, k], 1), :], sem.at[1]))

    def start(k, c):
        a, b = copies(k)
        a.start()
        b.start()
        return c

    def wait(k, c):
        a, b = copies(k)
        a.wait()
        b.wait()
        return c

    lax.fori_loop(0, bt, start, 0, unroll=DMA_UNROLL)
    lax.fori_loop(0, bt, wait, 0, unroll=DMA_UNROLL)


def _scatter_rows(hn, pos1, pos2, n_rows):
    t, d = hn.shape
    bt = _pick(t, 512)
    smem_spec = pl.BlockSpec((1, 1, bt), lambda i: (i, 0, 0), memory_space=pltpu.SMEM)
    any_spec = pl.BlockSpec(memory_space=pl.ANY)
    return pl.pallas_call(
        _scatter_kernel,
        grid=(t // bt,),
        in_specs=[smem_spec, smem_spec, pl.BlockSpec((bt, d), lambda i: (i, 0)), any_spec],
        out_specs=any_spec,
        out_shape=jax.ShapeDtypeStruct((n_rows, d), hn.dtype),
        scratch_shapes=[pltpu.SemaphoreType.DMA((2,))],
        input_output_aliases={3: 0},
        compiler_params=_params("arbitrary"),
        name="scatter_rows",
    )(pos1.reshape(t // bt, 1, bt), pos2.reshape(t // bt, 1, bt), hn, jnp.zeros((n_rows, d), hn.dtype))


def _grouped_ffn_kernel(be_ref, na_ref, hs_ref, wg_ref, wu_ref, wd_ref, out_ref, hb_ref, acc_ref):
    del be_ref
    i = pl.program_id(0)
    j = pl.program_id(1)
    active = i < na_ref[0]

    @pl.when(active & (j == 0))
    def _():
        hb_ref[...] = hs_ref[...].astype(BF16)
        acc_ref[...] = jnp.zeros_like(acc_ref)

    @pl.when(active)
    def _():
        _swiglu_step(hb_ref[...], wg_ref, wu_ref, wd_ref, acc_ref)

    @pl.when(j == pl.num_programs(1) - 1)
    def _():
        @pl.when(active)
        def _():
            out_ref[...] = acc_ref[...]

        @pl.when(jnp.logical_not(active))
        def _():
            out_ref[...] = jnp.zeros_like(out_ref)


def _grouped_ffn(hs, block_expert, n_active, wg, wu, wd, bm):
    p, d = hs.shape
    f = wg.shape[2]
    bf = _pick(f, 512)
    nf = f // bf

    def jj(i, j, na):
        return jnp.where(i < na[0], j, nf - 1)

    grid_spec = pltpu.PrefetchScalarGridSpec(
        num_scalar_prefetch=2,
        grid=(p // bm, nf),
        in_specs=[
            pl.BlockSpec((bm, d), lambda i, j, be, na: (jnp.maximum(jnp.minimum(i, na[0] - 1), 0), 0)),
            pl.BlockSpec((None, d, bf), lambda i, j, be, na: (be[i], 0, jj(i, j, na))),
            pl.BlockSpec((None, d, bf), lambda i, j, be, na: (be[i], 0, jj(i, j, na))),
            pl.BlockSpec((None, bf, d), lambda i, j, be, na: (be[i], jj(i, j, na), 0)),
        ],
        out_specs=pl.BlockSpec((bm, d), lambda i, j, be, na: (i, 0)),
        scratch_shapes=[pltpu.VMEM((bm, d), BF16), pltpu.VMEM((bm, d), F32)],
    )
    return pl.pallas_call(
        _grouped_ffn_kernel,
        grid_spec=grid_spec,
        out_shape=jax.ShapeDtypeStruct((p, d), F32),
        compiler_params=_params("arbitrary", "arbitrary"),
        name="grouped_ffn",
    )(block_expert, n_active, hs, wg, wu, wd)


def _combine_kernel(pos1_ref, pos2_ref, x_ref, info_ref, y_ref, out_ref, buf1_ref, buf2_ref, sem):
    bt = x_ref.shape[0]

    def copies(k):
        return (pltpu.make_async_copy(y_ref.at[pl.ds(pos1_ref[0, 0, k], 1), :],
                                      buf1_ref.at[pl.ds(k, 1), :], sem.at[0]),
                pltpu.make_async_copy(y_ref.at[pl.ds(pos2_ref[0, 0, k], 1), :],
                                      buf2_ref.at[pl.ds(k, 1), :], sem.at[1]))

    def start(k, c):
        a, b = copies(k)
        a.start()
        b.start()
        return c

    def wait(k, c):
        a, b = copies(k)
        a.wait()
        b.wait()
        return c

    lax.fori_loop(0, bt, start, 0, unroll=DMA_UNROLL)
    lax.fori_loop(0, bt, wait, 0, unroll=DMA_UNROLL)
    info = info_ref[...]
    g1 = info[:, INFO_G1:INFO_G1 + 1]
    g2 = info[:, INFO_G2:INFO_G2 + 1]
    out_ref[...] = x_ref[...] + g1 * buf1_ref[...] + g2 * buf2_ref[...]


def _combine(x2, info, y, pos1, pos2):
    t, d = x2.shape
    bt = _pick(t, 256)
    smem_spec = pl.BlockSpec((1, 1, bt), lambda i: (i, 0, 0), memory_space=pltpu.SMEM)
    return pl.pallas_call(
        _combine_kernel,
        grid=(t // bt,),
        in_specs=[
            smem_spec, smem_spec,
            pl.BlockSpec((bt, d), lambda i: (i, 0)),
            pl.BlockSpec((bt, LANES), lambda i: (i, 0)),
            pl.BlockSpec(memory_space=pl.ANY),
        ],
        out_specs=pl.BlockSpec((bt, d), lambda i: (i, 0)),
        out_shape=jax.ShapeDtypeStruct((t, d), F32),
        scratch_shapes=[pltpu.VMEM((bt, d), F32), pltpu.VMEM((bt, d), F32),
                        pltpu.SemaphoreType.DMA((2,))],
        compiler_params=_params("arbitrary"),
        name="combine",
    )(pos1.reshape(t // bt, 1, bt), pos2.reshape(t // bt, 1, bt), x2, info, y)


MOE_BLOCK_ROWS = 512


def _moe_ffn(x2, g, wr, wg, wu, wd):
    t, d = x2.shape
    bm = MOE_BLOCK_ROWS
    wr_pad = jnp.zeros((d, LANES), F32).at[:, :N_EXPERTS].set(wr)
    hn, info, cnt = _router(x2, g, wr_pad)
    e1 = info[:, INFO_E1].astype(jnp.int32)
    e2 = info[:, INFO_E2].astype(jnp.int32)
    r1 = info[:, INFO_R1].astype(jnp.int32)
    r2 = info[:, INFO_R2].astype(jnp.int32)
    counts = cnt[0, :N_EXPERTS].astype(jnp.int32)
    blocks_per_expert = (counts + bm - 1) // bm
    block_end = jnp.cumsum(blocks_per_expert)
    offsets = (block_end - blocks_per_expert) * bm
    pos1 = offsets[e1] + r1
    pos2 = offsets[e2] + r2
    n_blocks = (2 * t) // bm + N_EXPERTS
    n_active = block_end[-1:]
    ids = jnp.minimum(jnp.arange(n_blocks, dtype=jnp.int32), n_active[0] - 1)
    block_expert = jnp.sum((ids[:, None] >= block_end[None, :]).astype(jnp.int32), axis=1)
    block_expert = jnp.minimum(block_expert, N_EXPERTS - 1)
    hs = _scatter_rows(hn, pos1, pos2, n_blocks * bm)
    y = _grouped_ffn(hs, block_expert, n_active.astype(jnp.int32), wg, wu, wd, bm)
    return _combine(x2, info, y, pos1, pos2)


def _attention(proj_a, proj_b, batch, seq, shifted):
    views = [a.reshape(N_COLS_A, batch, seq, HEAD_DIM) for a in proj_a]
    branches = [_dilated_branch(v, batch, seq, dil, shifted) for v, dil in zip(views, DILATIONS)]
    ob = _gqa(proj_b, batch, seq, shifted)
    return [b[0] for b in branches], [b[1] for b in branches], ob


def kernel(x, g_mix, w_in, qn_a, kn_a, qn_b, kn_b, on_a, on_b, w_out, g_ffn, w_gate, w_up, w_down,
           w_router, we_gate, we_up, we_down):
    batch, seq, d = x.shape
    depth = w_in.shape[0]
    t = batch * seq
    assert all(w // (2 * dil) == BAND_HALF for w, dil in DILATED_PATTERNS) and DILATIONS == (1, 4, 16)
    assert w_in.shape[2] == (N_COLS_A + N_COLS_B) * HEAD_DIM
    scale = HEAD_DIM ** -0.5
    tables = _rope_tables(seq)
    x2 = x.reshape(t, d)
    for l in range(depth):
        gains = jnp.concatenate([jnp.tile(qn_a[l] * scale, N_HEADS_A), jnp.tile(kn_a[l], N_HEADS_A),
                                 jnp.ones((N_HEADS_A * HEAD_DIM,), F32),
                                 jnp.tile(qn_b[l] * scale, N_HEADS_B), jnp.tile(kn_b[l], N_KV_B),
                                 jnp.ones((N_KV_B * HEAD_DIM,), F32)])[None, :]
        bound = 1.02 * HEAD_DIM ** 0.5 * jnp.maximum(
            jnp.max(jnp.abs(qn_a[l])) * jnp.max(jnp.abs(kn_a[l])),
            jnp.max(jnp.abs(qn_b[l])) * jnp.max(jnp.abs(kn_b[l])))
        a1, a4, a16, proj_b = _inproj(x2, g_mix[l][None, :], w_in[l].astype(BF16), gains,
                                      tables[0] + tables[1], batch, seq)
        attend = functools.partial(_attention, (a1, a4, a16), proj_b, batch, seq)
        o_br, lse_br, ob = lax.cond(bound <= MAX_UNSHIFTED_SCORE, lambda: attend(False),
                                    lambda: attend(True))
        x2 = _outproj(o_br, lse_br, ob, on_a[l][None, :], on_b[l][None, :], w_out[l].astype(BF16),
                      x2, batch, seq)
        i = l // 2
        if l % 2 == 0:
            x2 = _dense_ffn(x2, g_ffn[l][None, :], w_gate[i].astype(BF16), w_up[i].astype(BF16),
                            w_down[i].astype(BF16))
        else:
            x2 = _moe_ffn(x2, g_ffn[l][None, :], w_router[i], we_gate[i].astype(BF16),
                          we_up[i].astype(BF16), we_down[i].astype(BF16))
    return x2.reshape(batch, seq, d)
```

```python
import functools

import jax
import jax.numpy as jnp
from jax import lax
from jax.experimental import pallas as pl
from jax.experimental.pallas import tpu as pltpu

HEAD_DIM = 128
N_HEADS_A = 8
N_HEADS_B = 8
N_KV_B = 2
GQA_REP = N_HEADS_B // N_KV_B
N_COLS_A = 3 * N_HEADS_A
N_COLS_B = N_HEADS_B + 2 * N_KV_B
DILATED_PATTERNS = ((128, 1), (512, 4), (2048, 16))
DILATIONS = tuple(d for _, d in DILATED_PATTERNS)
BAND_HALF = 64
GRID_W = 64
ROPE_THETA = 10000.0
N_EXPERTS = 8
EPS = 1e-6
NEG = -1e30
MAX_UNSHIFTED_SCORE = 40.0

HEADS_PER_STEP = 4
LANES = 128
DMA_UNROLL = 8
V7X_VMEM_LIMIT_BYTES = 56 * 1024 * 1024

BF16 = jnp.bfloat16
F32 = jnp.float32


def _params(*semantics):
    return pltpu.CompilerParams(dimension_semantics=semantics, vmem_limit_bytes=V7X_VMEM_LIMIT_BYTES)


def _pick(n, pref):
    b = min(n, pref)
    assert n % b == 0, (n, pref)
    return b


def _rope_angles(pos, dim):
    inv = ROPE_THETA ** (-jnp.arange(0, dim, 2, dtype=F32) / dim)
    ang = pos.astype(F32)[:, None] * inv[None, :]
    return jnp.cos(ang), jnp.sin(ang)


def _rope_tables(seq):
    pos = jnp.arange(seq)
    c1, s1 = _rope_angles(pos, HEAD_DIM)
    cos_a = jnp.concatenate([c1, c1], axis=-1)
    sin_a = jnp.concatenate([-s1, s1], axis=-1)
    cr, sr = _rope_angles(pos // GRID_W, HEAD_DIM // 2)
    cc, sc = _rope_angles(pos % GRID_W, HEAD_DIM // 2)
    z = jnp.zeros_like(sr)
    cos_b = jnp.concatenate([cr, cr, cc, cc], axis=-1)
    sin_b_up = jnp.concatenate([-sr, z, -sc, z], axis=-1)
    sin_b_dn = jnp.concatenate([z, sr, z, sc], axis=-1)
    return (cos_a, sin_a), (cos_b, sin_b_up, sin_b_dn)


def _rms_to_bf16(x_ref, g_ref, hn_ref):
    x = x_ref[...]
    ms = jnp.mean(x * x, axis=-1, keepdims=True)
    hn_ref[...] = (x * lax.rsqrt(ms + EPS) * g_ref[...]).astype(BF16)


def _head(y, hh):
    return y[:, hh * HEAD_DIM:(hh + 1) * HEAD_DIM]


def _inproj_kernel(x_ref, g_ref, w_ref, gain_ref, cosa_ref, sina_ref, cosb_ref, sinbu_ref, sinbd_ref,
                   a1_ref, a4_ref, a16_ref, b_ref, hn_ref, tmp_ref):
    bm = x_ref.shape[0]
    _rms_to_bf16(x_ref, g_ref, hn_ref)
    width = HEADS_PER_STEP * HEAD_DIM
    for c in range((N_COLS_A + N_COLS_B) // HEADS_PER_STEP):
        y = jnp.dot(hn_ref[...], w_ref[:, c * width:(c + 1) * width], preferred_element_type=F32)
        for hh in range(HEADS_PER_STEP):
            col = c * HEADS_PER_STEP + hh
            gain = gain_ref[:, col * HEAD_DIM:(col + 1) * HEAD_DIM]
            yh = _head(y, hh)
            is_a = col < N_COLS_A
            is_v = (2 * N_HEADS_A <= col < N_COLS_A) or col >= N_COLS_A + N_HEADS_B + N_KV_B
            if is_v:
                res = yh
            else:
                ms = jnp.mean(yh * yh, axis=-1, keepdims=True)
                yn = yh * lax.rsqrt(ms + EPS) * gain
                if is_a:
                    res = yn * cosa_ref[...] + pltpu.roll(yn, 64, 1) * sina_ref[...]
                else:
                    res = (yn * cosb_ref[...] + pltpu.roll(yn, 96, 1) * sinbu_ref[...]
                           + pltpu.roll(yn, 32, 1) * sinbd_ref[...])
            if is_a:
                a1_ref[col] = res.astype(BF16)
                tmp_ref[col] = res
                for out_ref, dil in ((a4_ref, 4), (a16_ref, 16)):
                    for r in range(dil):
                        out_ref[col, r] = tmp_ref[col, pl.ds(r, bm // dil, stride=dil), :].astype(BF16)
            else:
                b_ref[col - N_COLS_A] = res.astype(BF16)


def _inproj(x2, g, w_all, layer, gain_cols, tables, batch, seq):
    t, d = x2.shape
    bm = _pick(seq, 256)
    nps = seq // bm
    n = w_all.shape[2]
    tab_spec = pl.BlockSpec((bm, HEAD_DIM), lambda i: (i % nps, 0))

    def rm_spec(dil):
        return pl.BlockSpec((N_COLS_A, None, dil, bm // dil, HEAD_DIM),
                            lambda i: (0, i // nps, 0, i % nps, 0))

    def rm_shape(dil):
        return jax.ShapeDtypeStruct((N_COLS_A, batch, dil, seq // dil, HEAD_DIM), BF16)

    return pl.pallas_call(
        _inproj_kernel,
        grid=(t // bm,),
        in_specs=[
            pl.BlockSpec((bm, d), lambda i: (i, 0)),
            pl.BlockSpec((1, d), lambda i: (0, 0)),
            pl.BlockSpec((None, d, n), lambda i: (layer, 0, 0), pipeline_mode=pl.Buffered(1)),
            pl.BlockSpec((1, n), lambda i: (0, 0)),
            tab_spec, tab_spec, tab_spec, tab_spec, tab_spec,
        ],
        out_specs=[pl.BlockSpec((N_COLS_A, bm, HEAD_DIM), lambda i: (0, i, 0)), rm_spec(4), rm_spec(16),
                   pl.BlockSpec((N_COLS_B, bm, HEAD_DIM), lambda i: (0, i, 0))],
        out_shape=[jax.ShapeDtypeStruct((N_COLS_A, t, HEAD_DIM), BF16), rm_shape(4), rm_shape(16),
                   jax.ShapeDtypeStruct((N_COLS_B, t, HEAD_DIM), BF16)],
        scratch_shapes=[pltpu.VMEM((bm, d), BF16), pltpu.VMEM((N_COLS_A, bm, HEAD_DIM), F32)],
        compiler_params=_params("parallel"),
        name="inproj",
    )(x2, g, w_all, gain_cols, *tables)


QB_DIL = 128
KW_DIL = QB_DIL + 2 * BAND_HALF
DIL_UNROLL = 8


def _dilated_kernel(q_ref, k_ref, v_ref, o_ref, lse_ref, *, seg, shifted):
    h = pl.program_id(1)
    rows = q_ref.shape[0]
    rel0 = (lax.broadcasted_iota(jnp.int32, (QB_DIL, KW_DIL), 1)
            - lax.broadcasted_iota(jnp.int32, (QB_DIL, KW_DIL), 0))
    lane = lax.broadcasted_iota(jnp.int32, (QB_DIL, LANES), 1)

    @pl.when(h == 0)
    def _():
        lse_ref[...] = jnp.zeros_like(lse_ref)

    def body(blk, carry):
        l0 = pl.multiple_of(blk * QB_DIL, QB_DIL)
        seg0 = l0 & (-seg)
        w0 = pl.multiple_of(jnp.clip(l0 - BAND_HALF, seg0, seg0 + seg - KW_DIL), BAND_HALF)
        q = q_ref[pl.ds(l0, QB_DIL), :]
        k = k_ref[pl.ds(w0, KW_DIL), :]
        v = v_ref[pl.ds(w0, KW_DIL), :]
        s = lax.dot_general(q, k, (((1,), (1,)), ((), ())), preferred_element_type=F32)
        s = jnp.where(jnp.abs(rel0 + (w0 - l0)) <= BAND_HALF, s, NEG)
        if shifted:
            m = jnp.max(s, axis=-1, keepdims=True)
            p = jnp.exp(s - m)
        else:
            p = jnp.exp(s)
        den = jnp.sum(p, axis=-1, keepdims=True)
        o = jnp.dot(p.astype(BF16), v, preferred_element_type=F32) / den
        lse = jnp.log(den) + m if shifted else jnp.log(den)
        o_ref[pl.ds(l0, QB_DIL), :] = o.astype(BF16)
        lse_ref[pl.ds(l0, QB_DIL), :] = jnp.where(lane == h, lse, lse_ref[pl.ds(l0, QB_DIL), :])
        return carry

    lax.fori_loop(0, rows // QB_DIL, body, 0, unroll=DIL_UNROLL)


def _dilated_branch(qkv_rm, batch, seq, dil, shifted):
    seg = seq // dil
    assert seg >= KW_DIL and seg % QB_DIL == 0 and seg & (seg - 1) == 0

    def spec(off):
        return pl.BlockSpec((None, None, seq, HEAD_DIM), lambda b, h: (off + h, b, 0, 0))

    return pl.pallas_call(
        functools.partial(_dilated_kernel, seg=seg, shifted=shifted),
        grid=(batch, N_HEADS_A),
        in_specs=[spec(0), spec(N_HEADS_A), spec(2 * N_HEADS_A)],
        out_specs=[pl.BlockSpec((None, None, seq, HEAD_DIM), lambda b, h: (h, b, 0, 0)),
                   pl.BlockSpec((None, seq, LANES), lambda b, h: (b, 0, 0))],
        out_shape=[jax.ShapeDtypeStruct((N_HEADS_A, batch, seq, HEAD_DIM), BF16),
                   jax.ShapeDtypeStruct((batch, seq, LANES), F32)],
        compiler_params=_params("parallel", "arbitrary"),
        name=f"dilated_d{dil}" + ("_shifted" if shifted else ""),
    )(qkv_rm, qkv_rm, qkv_rm)


def _gqa_kernel(q_ref, k_ref, v_ref, o_ref, *, bk, shifted):
    rep, bq, dh = q_ref.shape
    seq = k_ref.shape[0]
    rows = rep * bq
    q = q_ref[...].reshape(rows, dh)

    def scores(c):
        c0 = pl.multiple_of(c * bk, bk)
        k = k_ref[pl.ds(c0, bk), :]
        v = v_ref[pl.ds(c0, bk), :]
        return lax.dot_general(q, k, (((1,), (1,)), ((), ())), preferred_element_type=F32), v

    if shifted:
        def body(c, carry):
            m, l, acc = carry
            s, v = scores(c)
            m_new = jnp.maximum(m, jnp.max(s, axis=-1, keepdims=True))
            alpha = jnp.exp(m - m_new)
            p = jnp.exp(s - m_new)
            l = alpha * l + jnp.sum(p, axis=-1, keepdims=True)
            acc = alpha * acc + jnp.dot(p.astype(BF16), v, preferred_element_type=F32)
            return m_new, l, acc

        init = (jnp.full((rows, 1), NEG, F32), jnp.zeros((rows, 1), F32), jnp.zeros((rows, dh), F32))
        _, l, acc = lax.fori_loop(0, seq // bk, body, init)
    else:
        def body(c, carry):
            lsum, acc = carry
            s, v = scores(c)
            p = jnp.exp(s)
            for tile in range(bk // LANES):
                lsum = lsum + p[:, tile * LANES:(tile + 1) * LANES]
            acc = acc + jnp.dot(p.astype(BF16), v, preferred_element_type=F32)
            return lsum, acc

        init = (jnp.zeros((rows, LANES), F32), jnp.zeros((rows, dh), F32))
        lsum, acc = lax.fori_loop(0, seq // bk, body, init, unroll=4)
        l = jnp.sum(lsum, axis=-1, keepdims=True)
    o = acc / l
    for h in range(rep):
        o_ref[:, h * dh:(h + 1) * dh] = o[h * bq:(h + 1) * bq].astype(o_ref.dtype)


def _gqa(proj_b, batch, seq, shifted):
    t = batch * seq
    bq = _pick(seq, 256)
    bk = _pick(seq, 512)
    nq = seq // bq
    view = proj_b.reshape(N_COLS_B, batch, seq, HEAD_DIM)
    return pl.pallas_call(
        functools.partial(_gqa_kernel, bk=bk, shifted=shifted),
        grid=(batch, N_KV_B, nq),
        in_specs=[
            pl.BlockSpec((GQA_REP, None, bq, HEAD_DIM), lambda b, g, i: (g, b, i, 0)),
            pl.BlockSpec((None, None, seq, HEAD_DIM), lambda b, g, i: (N_HEADS_B + g, b, 0, 0)),
            pl.BlockSpec((None, None, seq, HEAD_DIM), lambda b, g, i: (N_HEADS_B + N_KV_B + g, b, 0, 0)),
        ],
        out_specs=pl.BlockSpec((bq, GQA_REP * HEAD_DIM), lambda b, g, i: (b * nq + i, g)),
        out_shape=jax.ShapeDtypeStruct((t, N_HEADS_B * HEAD_DIM), BF16),
        compiler_params=_params("parallel", "parallel", "arbitrary"),
        name="gqa" + ("_shifted" if shifted else ""),
    )(view, view, view)


def _outproj_kernel(o1_ref, o4_ref, o16_ref, l1_ref, l4_ref, l16_ref, ob_ref, ona_ref, onb_ref,
                    w_ref, x_ref, out_ref, mixed_ref, oa_ref, t4_ref, t16_ref):
    wa = N_HEADS_A * HEAD_DIM
    wb = ob_ref.shape[1]
    bm = oa_ref.shape[0]

    def to_token_order(tmp_ref, slot, load_residue, dil):
        for r in range(dil):
            tmp_ref[slot, pl.ds(r, bm // dil, stride=dil), :] = load_residue(r)
        return tmp_ref[slot]

    l1 = l1_ref[...]
    l4 = to_token_order(t4_ref, N_HEADS_A, lambda r: l4_ref[r], 4)
    l16 = to_token_order(t16_ref, N_HEADS_A, lambda r: l16_ref[r], 16)
    mx = jnp.maximum(jnp.maximum(l1, l4), l16)
    e1, e4, e16 = jnp.exp(l1 - mx), jnp.exp(l4 - mx), jnp.exp(l16 - mx)
    den = e1 + e4 + e16
    w1, w4, w16 = e1 / den, e4 / den, e16 / den
    ss = jnp.zeros((bm, HEAD_DIM), F32)
    for h in range(N_HEADS_A):
        o4 = to_token_order(t4_ref, h, lambda r: o4_ref[h, r].astype(F32), 4)
        o16 = to_token_order(t16_ref, h, lambda r: o16_ref[h, r].astype(F32), 16)
        oa = (w1[:, h:h + 1] * o1_ref[h].astype(F32) + w4[:, h:h + 1] * o4
              + w16[:, h:h + 1] * o16)
        oa_ref[:, h * HEAD_DIM:(h + 1) * HEAD_DIM] = oa
        ss = ss + oa * oa
    ra = lax.rsqrt(jnp.sum(ss, axis=-1, keepdims=True) / wa + EPS)
    mixed_ref[:, :wa] = (oa_ref[...] * ra * ona_ref[...]).astype(BF16)
    ob = ob_ref[...].astype(F32)
    rb = lax.rsqrt(jnp.mean(ob * ob, axis=-1, keepdims=True) + EPS)
    mixed_ref[:, wa:wa + wb] = (ob * rb * onb_ref[...]).astype(BF16)
    out_ref[...] = x_ref[...] + jnp.dot(mixed_ref[...], w_ref[...], preferred_element_type=F32)


def _outproj(o_branches, lse_branches, ob, on_a, on_b, w_all, layer, x2, batch, seq):
    t, d = x2.shape
    bm = _pick(seq, 256)
    nps = seq // bm
    wa = N_HEADS_A * HEAD_DIM
    wb = ob.shape[1]
    o1, o4, o16 = o_branches
    l1, l4, l16 = lse_branches
    o1 = o1.reshape(N_HEADS_A, t, HEAD_DIM)
    l1 = l1.reshape(t, LANES)

    def o_spec(dil):
        return pl.BlockSpec((N_HEADS_A, None, dil, bm // dil, HEAD_DIM),
                            lambda i: (0, i // nps, 0, i % nps, 0))

    def l_spec(dil):
        return pl.BlockSpec((None, dil, bm // dil, LANES), lambda i: (i // nps, 0, i % nps, 0))

    def rm(a, dil):
        return a.reshape(a.shape[:-2] + (dil, seq // dil, a.shape[-1]))

    return pl.pallas_call(
        _outproj_kernel,
        grid=(t // bm,),
        in_specs=[
            pl.BlockSpec((N_HEADS_A, bm, HEAD_DIM), lambda i: (0, i, 0)), o_spec(4), o_spec(16),
            pl.BlockSpec((bm, LANES), lambda i: (i, 0)), l_spec(4), l_spec(16),
            pl.BlockSpec((bm, wb), lambda i: (i, 0)),
            pl.BlockSpec((1, wa), lambda i: (0, 0)),
            pl.BlockSpec((1, wb), lambda i: (0, 0)),
            pl.BlockSpec((None, wa + wb, d), lambda i: (layer, 0, 0), pipeline_mode=pl.Buffered(1)),
            pl.BlockSpec((bm, d), lambda i: (i, 0)),
        ],
        out_specs=pl.BlockSpec((bm, d), lambda i: (i, 0)),
        out_shape=jax.ShapeDtypeStruct((t, d), F32),
        scratch_shapes=[pltpu.VMEM((bm, wa + wb), BF16), pltpu.VMEM((bm, wa), F32),
                        pltpu.VMEM((N_HEADS_A + 1, bm, HEAD_DIM), F32),
                        pltpu.VMEM((N_HEADS_A + 1, bm, HEAD_DIM), F32)],
        compiler_params=_params("parallel"),
        name="outproj",
    )(o1, rm(o4, 4), rm(o16, 16), l1, rm(l4, 4), rm(l16, 16), ob, on_a, on_b, w_all, x2)


def _swiglu_step(h, wg_ref, wu_ref, wd_ref, acc_ref):
    g = jnp.dot(h, wg_ref[...], preferred_element_type=F32)
    u = jnp.dot(h, wu_ref[...], preferred_element_type=F32)
    a = (g * jax.nn.sigmoid(g) * u).astype(BF16)
    acc_ref[...] += jnp.dot(a, wd_ref[...], preferred_element_type=F32)


def _dense_ffn_kernel(x_ref, g_ref, wg_ref, wu_ref, wd_ref, out_ref, hn_ref, acc_ref):
    j = pl.program_id(1)

    @pl.when(j == 0)
    def _():
        _rms_to_bf16(x_ref, g_ref, hn_ref)
        acc_ref[...] = jnp.zeros_like(acc_ref)

    _swiglu_step(hn_ref[...], wg_ref, wu_ref, wd_ref, acc_ref)

    @pl.when(j == pl.num_programs(1) - 1)
    def _():
        out_ref[...] = x_ref[...] + acc_ref[...]


def _dense_ffn(x2, g, wg, wu, wd, li):
    t, d = x2.shape
    f = wg.shape[2]
    bm = _pick(t, 512)
    bf = _pick(f, 512)
    return pl.pallas_call(
        _dense_ffn_kernel,
        grid=(t // bm, f // bf),
        in_specs=[
            pl.BlockSpec((bm, d), lambda i, j: (i, 0)),
            pl.BlockSpec((1, d), lambda i, j: (0, 0)),
            pl.BlockSpec((None, d, bf), lambda i, j: (li, 0, j)),
            pl.BlockSpec((None, d, bf), lambda i, j: (li, 0, j)),
            pl.BlockSpec((None, bf, d), lambda i, j: (li, j, 0)),
        ],
        out_specs=pl.BlockSpec((bm, d), lambda i, j: (i, 0)),
        out_shape=jax.ShapeDtypeStruct((t, d), F32),
        scratch_shapes=[pltpu.VMEM((bm, d), BF16), pltpu.VMEM((bm, d), F32)],
        compiler_params=_params("parallel", "arbitrary"),
        name="dense_ffn",
    )(x2, g, wg, wu, wd)


INFO_E1, INFO_E2, INFO_R1, INFO_R2, INFO_G1, INFO_G2 = range(6)


def _router_kernel(x_ref, g_ref, wr_ref, hn_ref, info_ref, cnt_ref, run_ref):
    i = pl.program_id(0)
    bm = x_ref.shape[0]

    @pl.when(i == 0)
    def _():
        run_ref[...] = jnp.zeros_like(run_ref)

    x = x_ref[...]
    ms = jnp.mean(x * x, axis=-1, keepdims=True)
    hn = x * lax.rsqrt(ms + EPS) * g_ref[...]
    hn_ref[...] = hn
    logits = jnp.dot(hn, wr_ref[...], preferred_element_type=F32, precision=lax.Precision.HIGHEST)
    lane = lax.broadcasted_iota(jnp.int32, (bm, LANES), 1)
    lg = jnp.where(lane < N_EXPERTS, logits, -jnp.inf)
    v1 = jnp.max(lg, axis=-1, keepdims=True)
    i1 = jnp.min(jnp.where(lg == v1, lane, LANES), axis=-1, keepdims=True)
    lg2 = jnp.where(lane == i1, -jnp.inf, lg)
    v2 = jnp.max(lg2, axis=-1, keepdims=True)
    i2 = jnp.min(jnp.where(lg2 == v2, lane, LANES), axis=-1, keepdims=True)
    e = jnp.exp(v2 - v1)
    g1 = 1.0 / (1.0 + e)
    g2 = e / (1.0 + e)
    hot1 = lane == i1
    hot2 = lane == i2
    onehot = (hot1 | hot2).astype(F32)
    tri = (lax.broadcasted_iota(jnp.int32, (bm, bm), 1)
           < lax.broadcasted_iota(jnp.int32, (bm, bm), 0)).astype(BF16)
    rank = jnp.dot(tri, onehot.astype(BF16), preferred_element_type=F32) + run_ref[0:1, :]
    r1 = jnp.sum(jnp.where(hot1, rank, 0.0), axis=-1, keepdims=True)
    r2 = jnp.sum(jnp.where(hot2, rank, 0.0), axis=-1, keepdims=True)
    run_ref[...] = run_ref[...] + jnp.sum(onehot, axis=0, keepdims=True)
    info = jnp.zeros((bm, LANES), F32)
    for col, val in ((INFO_E1, i1.astype(F32)), (INFO_E2, i2.astype(F32)), (INFO_R1, r1),
                     (INFO_R2, r2), (INFO_G1, g1), (INFO_G2, g2)):
        info = jnp.where(lane == col, val, info)
    info_ref[...] = info
    cnt_ref[...] = run_ref[...]


def _router(x2, g, wr_pad):
    t, d = x2.shape
    bm = _pick(t, 512)
    return pl.pallas_call(
        _router_kernel,
        grid=(t // bm,),
        in_specs=[
            pl.BlockSpec((bm, d), lambda i: (i, 0)),
            pl.BlockSpec((1, d), lambda i: (0, 0)),
            pl.BlockSpec((d, LANES), lambda i: (0, 0)),
        ],
        out_specs=[
            pl.BlockSpec((bm, d), lambda i: (i, 0)),
            pl.BlockSpec((bm, LANES), lambda i: (i, 0)),
            pl.BlockSpec((8, LANES), lambda i: (0, 0)),
        ],
        out_shape=[jax.ShapeDtypeStruct((t, d), F32), jax.ShapeDtypeStruct((t, LANES), F32),
                   jax.ShapeDtypeStruct((8, LANES), F32)],
        scratch_shapes=[pltpu.VMEM((8, LANES), F32)],
        compiler_params=_params("arbitrary"),
        name="router",
    )(x2, g, wr_pad)


def _scatter_kernel(pos1_ref, pos2_ref, hn_ref, hs_in_ref, hs_ref, sem):
    del hs_in_ref
    bt = hn_ref.shape[0]

    def copies(k):
        src = hn_ref.at[pl.ds(k, 1), :]
        return (pltpu.make_async_copy(src, hs_ref.at[pl.ds(pos1_ref[0, 0, k], 1), :], sem.at[0]),
                pltpu.make_async_copy(src, hs_ref.at[pl.ds(pos2_ref[0, 0, k], 1), :], sem.at[1]))

    def start(k, c):
        a, b = copies(k)
        a.start()
        b.start()
        return c

    def wait(k, c):
        a, b = copies(k)
        a.wait()
        b.wait()
        return c

    lax.fori_loop(0, bt, start, 0, unroll=DMA_UNROLL)
    lax.fori_loop(0, bt, wait, 0, unroll=DMA_UNROLL)


def _scatter_rows(hn, pos1, pos2, n_rows):
    t, d = hn.shape
    bt = _pick(t, 512)
    smem_spec = pl.BlockSpec((1, 1, bt), lambda i: (i, 0, 0), memory_space=pltpu.SMEM)
    any_spec = pl.BlockSpec(memory_space=pl.ANY)
    return pl.pallas_call(
        _scatter_kernel,
        grid=(t // bt,),
        in_specs=[smem_spec, smem_spec, pl.BlockSpec((bt, d), lambda i: (i, 0)), any_spec],
        out_specs=any_spec,
        out_shape=jax.ShapeDtypeStruct((n_rows, d), hn.dtype),
        scratch_shapes=[pltpu.SemaphoreType.DMA((2,))],
        input_output_aliases={3: 0},
        compiler_params=_params("arbitrary"),
        name="scatter_rows",
    )(pos1.reshape(t // bt, 1, bt), pos2.reshape(t // bt, 1, bt), hn, jnp.zeros((n_rows, d), hn.dtype))


def _grouped_ffn_kernel(be_ref, na_ref, hs_ref, wg_ref, wu_ref, wd_ref, out_ref, hb_ref, acc_ref):
    del be_ref
    i = pl.program_id(0)
    j = pl.program_id(1)
    active = i < na_ref[0]

    @pl.when(active & (j == 0))
    def _():
        hb_ref[...] = hs_ref[...].astype(BF16)
        acc_ref[...] = jnp.zeros_like(acc_ref)

    @pl.when(active)
    def _():
        _swiglu_step(hb_ref[...], wg_ref, wu_ref, wd_ref, acc_ref)

    @pl.when(j == pl.num_programs(1) - 1)
    def _():
        @pl.when(active)
        def _():
            out_ref[...] = acc_ref[...]

        @pl.when(jnp.logical_not(active))
        def _():
            out_ref[...] = jnp.zeros_like(out_ref)


def _grouped_ffn(hs, block_expert, n_active, wg, wu, wd, li, bm):
    p, d = hs.shape
    f = wg.shape[3]
    bf = _pick(f, 512)
    nf = f // bf

    def jj(i, j, na):
        return jnp.where(i < na[0], j, nf - 1)

    grid_spec = pltpu.PrefetchScalarGridSpec(
        num_scalar_prefetch=2,
        grid=(p // bm, nf),
        in_specs=[
            pl.BlockSpec((bm, d), lambda i, j, be, na: (jnp.maximum(jnp.minimum(i, na[0] - 1), 0), 0)),
            pl.BlockSpec((None, None, d, bf), lambda i, j, be, na: (li, be[i], 0, jj(i, j, na))),
            pl.BlockSpec((None, None, d, bf), lambda i, j, be, na: (li, be[i], 0, jj(i, j, na))),
            pl.BlockSpec((None, None, bf, d), lambda i, j, be, na: (li, be[i], jj(i, j, na), 0)),
        ],
        out_specs=pl.BlockSpec((bm, d), lambda i, j, be, na: (i, 0)),
        scratch_shapes=[pltpu.VMEM((bm, d), BF16), pltpu.VMEM((bm, d), F32)],
    )
    return pl.pallas_call(
        _grouped_ffn_kernel,
        grid_spec=grid_spec,
        out_shape=jax.ShapeDtypeStruct((p, d), F32),
        compiler_params=_params("arbitrary", "arbitrary"),
        name="grouped_ffn",
    )(block_expert, n_active, hs, wg, wu, wd)


def _combine_kernel(pos1_ref, pos2_ref, x_ref, info_ref, y_ref, out_ref, buf1_ref, buf2_ref, sem):
    bt = x_ref.shape[0]

    def copies(k):
        return (pltpu.make_async_copy(y_ref.at[pl.ds(pos1_ref[0, 0, k], 1), :],
                                      buf1_ref.at[pl.ds(k, 1), :], sem.at[0]),
                pltpu.make_async_copy(y_ref.at[pl.ds(pos2_ref[0, 0, k], 1), :],
                                      buf2_ref.at[pl.ds(k, 1), :], sem.at[1]))

    def start(k, c):
        a, b = copies(k)
        a.start()
        b.start()
        return c

    def wait(k, c):
        a, b = copies(k)
        a.wait()
        b.wait()
        return c

    lax.fori_loop(0, bt, start, 0, unroll=DMA_UNROLL)
    lax.fori_loop(0, bt, wait, 0, unroll=DMA_UNROLL)
    info = info_ref[...]
    g1 = info[:, INFO_G1:INFO_G1 + 1]
    g2 = info[:, INFO_G2:INFO_G2 + 1]
    out_ref[...] = x_ref[...] + g1 * buf1_ref[...] + g2 * buf2_ref[...]


def _combine(x2, info, y, pos1, pos2):
    t, d = x2.shape
    bt = _pick(t, 256)
    smem_spec = pl.BlockSpec((1, 1, bt), lambda i: (i, 0, 0), memory_space=pltpu.SMEM)
    return pl.pallas_call(
        _combine_kernel,
        grid=(t // bt,),
        in_specs=[
            smem_spec, smem_spec,
            pl.BlockSpec((bt, d), lambda i: (i, 0)),
            pl.BlockSpec((bt, LANES), lambda i: (i, 0)),
            pl.BlockSpec(memory_space=pl.ANY),
        ],
        out_specs=pl.BlockSpec((bt, d), lambda i: (i, 0)),
        out_shape=jax.ShapeDtypeStruct((t, d), F32),
        scratch_shapes=[pltpu.VMEM((bt, d), F32), pltpu.VMEM((bt, d), F32),
                        pltpu.SemaphoreType.DMA((2,))],
        compiler_params=_params("arbitrary"),
        name="combine",
    )(pos1.reshape(t // bt, 1, bt), pos2.reshape(t // bt, 1, bt), x2, info, y)


MOE_BLOCK_ROWS = 512


def _moe_ffn(x2, g, wr, wg, wu, wd, li):
    t, d = x2.shape
    bm = MOE_BLOCK_ROWS
    wr_pad = jnp.zeros((d, LANES), F32).at[:, :N_EXPERTS].set(wr)
    hn, info, cnt = _router(x2, g, wr_pad)
    e1 = info[:, INFO_E1].astype(jnp.int32)
    e2 = info[:, INFO_E2].astype(jnp.int32)
    r1 = info[:, INFO_R1].astype(jnp.int32)
    r2 = info[:, INFO_R2].astype(jnp.int32)
    counts = cnt[0, :N_EXPERTS].astype(jnp.int32)
    blocks_per_expert = (counts + bm - 1) // bm
    block_end = jnp.cumsum(blocks_per_expert)
    offsets = (block_end - blocks_per_expert) * bm
    pos1 = offsets[e1] + r1
    pos2 = offsets[e2] + r2
    n_blocks = (2 * t) // bm + N_EXPERTS
    n_active = block_end[-1:]
    ids = jnp.minimum(jnp.arange(n_blocks, dtype=jnp.int32), n_active[0] - 1)
    block_expert = jnp.sum((ids[:, None] >= block_end[None, :]).astype(jnp.int32), axis=1)
    block_expert = jnp.minimum(block_expert, N_EXPERTS - 1)
    hs = _scatter_rows(hn, pos1, pos2, n_blocks * bm)
    y = _grouped_ffn(hs, block_expert, n_active.astype(jnp.int32), wg, wu, wd, li, bm)
    return _combine(x2, info, y, pos1, pos2)


def _attention(proj_a, proj_b, batch, seq, shifted):
    views = [a.reshape(N_COLS_A, batch, seq, HEAD_DIM) for a in proj_a]
    branches = [_dilated_branch(v, batch, seq, dil, shifted) for v, dil in zip(views, DILATIONS)]
    ob = _gqa(proj_b, batch, seq, shifted)
    return [b[0] for b in branches], [b[1] for b in branches], ob


def kernel(x, g_mix, w_in, qn_a, kn_a, qn_b, kn_b, on_a, on_b, w_out, g_ffn, w_gate, w_up, w_down,
           w_router, we_gate, we_up, we_down):
    batch, seq, d = x.shape
    depth = w_in.shape[0]
    t = batch * seq
    assert all(w // (2 * dil) == BAND_HALF for w, dil in DILATED_PATTERNS) and DILATIONS == (1, 4, 16)
    assert w_in.shape[2] == (N_COLS_A + N_COLS_B) * HEAD_DIM
    scale = HEAD_DIM ** -0.5
    tables = _rope_tables(seq)
    w_in_b, w_out_b = w_in.astype(BF16), w_out.astype(BF16)
    w_gate_b, w_up_b, w_down_b = w_gate.astype(BF16), w_up.astype(BF16), w_down.astype(BF16)
    we_gate_b, we_up_b, we_down_b = we_gate.astype(BF16), we_up.astype(BF16), we_down.astype(BF16)
    x2 = x.reshape(t, d)
    for l in range(depth):
        gains = jnp.concatenate([jnp.tile(qn_a[l] * scale, N_HEADS_A), jnp.tile(kn_a[l], N_HEADS_A),
                                 jnp.ones((N_HEADS_A * HEAD_DIM,), F32),
                                 jnp.tile(qn_b[l] * scale, N_HEADS_B), jnp.tile(kn_b[l], N_KV_B),
                                 jnp.ones((N_KV_B * HEAD_DIM,), F32)])[None, :]
        bound = 1.02 * HEAD_DIM ** 0.5 * jnp.maximum(
            jnp.max(jnp.abs(qn_a[l])) * jnp.max(jnp.abs(kn_a[l])),
            jnp.max(jnp.abs(qn_b[l])) * jnp.max(jnp.abs(kn_b[l])))
        a1, a4, a16, proj_b = _inproj(x2, g_mix[l][None, :], w_in_b, l, gains,
                                      tables[0] + tables[1], batch, seq)
        attend = functools.partial(_attention, (a1, a4, a16), proj_b, batch, seq)
        o_br, lse_br, ob = lax.cond(bound <= MAX_UNSHIFTED_SCORE, lambda: attend(False),
                                    lambda: attend(True))
        x2 = _outproj(o_br, lse_br, ob, on_a[l][None, :], on_b[l][None, :], w_out_b, l, x2, batch, seq)
        i = l // 2
        if l % 2 == 0:
            x2 = _dense_ffn(x2, g_ffn[l][None, :], w_gate_b, w_up_b, w_down_b, i)
        else:
            x2 = _moe_ffn(x2, g_ffn[l][None, :], w_router[i], we_gate_b, we_up_b, we_down_b, i)
    return x2.reshape(batch, seq, d)
```

```python
import functools

import jax
import jax.numpy as jnp
from jax import lax
from jax.experimental import pallas as pl
from jax.experimental.pallas import tpu as pltpu

HEAD_DIM = 128
N_HEADS_A = 8
N_HEADS_B = 8
N_KV_B = 2
GQA_REP = N_HEADS_B // N_KV_B
N_COLS_A = 3 * N_HEADS_A
N_COLS_B = N_HEADS_B + 2 * N_KV_B
DILATED_PATTERNS = ((128, 1), (512, 4), (2048, 16))
DILATIONS = tuple(d for _, d in DILATED_PATTERNS)
BAND_HALF = 64
GRID_W = 64
ROPE_THETA = 10000.0
N_EXPERTS = 8
EPS = 1e-6
NEG = -1e30
MAX_UNSHIFTED_SCORE = 40.0
LOG2_E = 1.4426950408889634

HEADS_PER_STEP = 4
LANES = 128
DMA_UNROLL = 8
V7X_VMEM_LIMIT_BYTES = 56 * 1024 * 1024

BF16 = jnp.bfloat16
F32 = jnp.float32


def _params(*semantics):
    return pltpu.CompilerParams(dimension_semantics=semantics, vmem_limit_bytes=V7X_VMEM_LIMIT_BYTES)


def _pick(n, pref):
    b = min(n, pref)
    assert n % b == 0, (n, pref)
    return b


def _rope_angles(pos, dim):
    inv = ROPE_THETA ** (-jnp.arange(0, dim, 2, dtype=F32) / dim)
    ang = pos.astype(F32)[:, None] * inv[None, :]
    return jnp.cos(ang), jnp.sin(ang)


def _rope_tables(seq):
    pos = jnp.arange(seq)
    c1, s1 = _rope_angles(pos, HEAD_DIM)
    cos_a = jnp.concatenate([c1, c1], axis=-1)
    sin_a = jnp.concatenate([-s1, s1], axis=-1)
    cr, sr = _rope_angles(pos // GRID_W, HEAD_DIM // 2)
    cc, sc = _rope_angles(pos % GRID_W, HEAD_DIM // 2)
    z = jnp.zeros_like(sr)
    cos_b = jnp.concatenate([cr, cr, cc, cc], axis=-1)
    sin_b_up = jnp.concatenate([-sr, z, -sc, z], axis=-1)
    sin_b_dn = jnp.concatenate([z, sr, z, sc], axis=-1)
    return (cos_a, sin_a), (cos_b, sin_b_up, sin_b_dn)


def _rms_to_bf16(x_ref, g_ref, hn_ref):
    x = x_ref[...]
    ms = jnp.mean(x * x, axis=-1, keepdims=True)
    hn_ref[...] = (x * lax.rsqrt(ms + EPS) * g_ref[...]).astype(BF16)


def _head(y, hh):
    return y[:, hh * HEAD_DIM:(hh + 1) * HEAD_DIM]


def _inproj_kernel(x_ref, g_ref, w_ref, gain_ref, cosa_ref, sina_ref, cosb_ref, sinbu_ref, sinbd_ref,
                   a1_ref, a4_ref, a16_ref, b_ref, hn_ref, tmp_ref):
    bm = x_ref.shape[0]
    _rms_to_bf16(x_ref, g_ref, hn_ref)
    width = HEADS_PER_STEP * HEAD_DIM
    for c in range((N_COLS_A + N_COLS_B) // HEADS_PER_STEP):
        y = jnp.dot(hn_ref[...], w_ref[:, c * width:(c + 1) * width], preferred_element_type=F32)
        for hh in range(HEADS_PER_STEP):
            col = c * HEADS_PER_STEP + hh
            gain = gain_ref[:, col * HEAD_DIM:(col + 1) * HEAD_DIM]
            yh = _head(y, hh)
            is_a = col < N_COLS_A
            is_v = (2 * N_HEADS_A <= col < N_COLS_A) or col >= N_COLS_A + N_HEADS_B + N_KV_B
            if is_v:
                res = yh
            else:
                ms = jnp.mean(yh * yh, axis=-1, keepdims=True)
                yn = yh * lax.rsqrt(ms + EPS) * gain
                if is_a:
                    res = yn * cosa_ref[...] + pltpu.roll(yn, 64, 1) * sina_ref[...]
                else:
                    res = (yn * cosb_ref[...] + pltpu.roll(yn, 96, 1) * sinbu_ref[...]
                           + pltpu.roll(yn, 32, 1) * sinbd_ref[...])
            if is_a:
                a1_ref[col] = res.astype(BF16)
                tmp_ref[col] = res
                for out_ref, dil in ((a4_ref, 4), (a16_ref, 16)):
                    for r in range(dil):
                        out_ref[col, r] = tmp_ref[col, pl.ds(r, bm // dil, stride=dil), :].astype(BF16)
            else:
                b_ref[col - N_COLS_A] = res.astype(BF16)


def _inproj(x2, g, w_all, layer, gain_cols, tables, batch, seq):
    t, d = x2.shape
    bm = _pick(seq, 256)
    nps = seq // bm
    n = w_all.shape[2]
    tab_spec = pl.BlockSpec((bm, HEAD_DIM), lambda i: (i % nps, 0))

    def rm_spec(dil):
        return pl.BlockSpec((N_COLS_A, None, dil, bm // dil, HEAD_DIM),
                            lambda i: (0, i // nps, 0, i % nps, 0))

    def rm_shape(dil):
        return jax.ShapeDtypeStruct((N_COLS_A, batch, dil, seq // dil, HEAD_DIM), BF16)

    return pl.pallas_call(
        _inproj_kernel,
        grid=(t // bm,),
        in_specs=[
            pl.BlockSpec((bm, d), lambda i: (i, 0)),
            pl.BlockSpec((1, d), lambda i: (0, 0)),
            pl.BlockSpec((None, d, n), lambda i: (layer, 0, 0), pipeline_mode=pl.Buffered(1)),
            pl.BlockSpec((1, n), lambda i: (0, 0)),
            tab_spec, tab_spec, tab_spec, tab_spec, tab_spec,
        ],
        out_specs=[pl.BlockSpec((N_COLS_A, bm, HEAD_DIM), lambda i: (0, i, 0)), rm_spec(4), rm_spec(16),
                   pl.BlockSpec((N_COLS_B, bm, HEAD_DIM), lambda i: (0, i, 0))],
        out_shape=[jax.ShapeDtypeStruct((N_COLS_A, t, HEAD_DIM), BF16), rm_shape(4), rm_shape(16),
                   jax.ShapeDtypeStruct((N_COLS_B, t, HEAD_DIM), BF16)],
        scratch_shapes=[pltpu.VMEM((bm, d), BF16), pltpu.VMEM((N_COLS_A, bm, HEAD_DIM), F32)],
        compiler_params=_params("parallel"),
        name="inproj",
    )(x2, g, w_all, gain_cols, *tables)


QB_DIL = 128
KW_DIL = QB_DIL + 2 * BAND_HALF
BAND_SHIFT = BAND_HALF.bit_length() - 1
assert 1 << BAND_SHIFT == BAND_HALF
N_WINDOW_PLACEMENTS = QB_DIL // BAND_HALF + 1
DIL_UNROLL = 8


def _dilated_kernel(q_ref, k_ref, v_ref, o_ref, lse_ref, bias_ref, *, seg, shifted):
    h = pl.program_id(1)
    rows = q_ref.shape[0]
    lane = lax.broadcasted_iota(jnp.int32, (QB_DIL, LANES), 1)

    @pl.when(h == 0)
    def _():
        lse_ref[...] = jnp.zeros_like(lse_ref)
        rel0 = (lax.broadcasted_iota(jnp.int32, (QB_DIL, KW_DIL), 1)
                - lax.broadcasted_iota(jnp.int32, (QB_DIL, KW_DIL), 0))
        for c in range(bias_ref.shape[0]):
            bias_ref[c] = jnp.where(jnp.abs(rel0 - BAND_HALF * c) <= BAND_HALF, 0.0, NEG)

    def body(blk, carry):
        l0 = pl.multiple_of(blk * QB_DIL, QB_DIL)
        seg0 = l0 & (-seg)
        w0 = pl.multiple_of(jnp.clip(l0 - BAND_HALF, seg0, seg0 + seg - KW_DIL), BAND_HALF)
        q = q_ref[pl.ds(l0, QB_DIL), :]
        k = k_ref[pl.ds(w0, KW_DIL), :]
        v = v_ref[pl.ds(w0, KW_DIL), :]
        s = lax.dot_general(q, k, (((1,), (1,)), ((), ())), preferred_element_type=F32)
        s = s + bias_ref[lax.shift_right_logical(l0 - w0, BAND_SHIFT)]
        if shifted:
            m = jnp.max(s, axis=-1, keepdims=True)
            p = jnp.exp2(s - m)
        else:
            p = jnp.exp2(s)
        den = jnp.sum(p, axis=-1, keepdims=True)
        o = jnp.dot(p.astype(BF16), v, preferred_element_type=F32) / den
        lse = jnp.log2(den) + m if shifted else jnp.log2(den)
        o_ref[pl.ds(l0, QB_DIL), :] = o.astype(BF16)
        lse_ref[pl.ds(l0, QB_DIL), :] = jnp.where(lane == h, lse, lse_ref[pl.ds(l0, QB_DIL), :])
        return carry

    lax.fori_loop(0, rows // QB_DIL, body, 0, unroll=DIL_UNROLL)


def _dilated_branch(qkv_rm, batch, seq, dil, shifted):
    seg = seq // dil
    assert seg >= KW_DIL and seg % QB_DIL == 0 and seg & (seg - 1) == 0

    def spec(off):
        return pl.BlockSpec((None, None, seq, HEAD_DIM), lambda b, h: (off + h, b, 0, 0))

    return pl.pallas_call(
        functools.partial(_dilated_kernel, seg=seg, shifted=shifted),
        grid=(batch, N_HEADS_A),
        in_specs=[spec(0), spec(N_HEADS_A), spec(2 * N_HEADS_A)],
        out_specs=[pl.BlockSpec((None, None, seq, HEAD_DIM), lambda b, h: (h, b, 0, 0)),
                   pl.BlockSpec((None, seq, LANES), lambda b, h: (b, 0, 0))],
        out_shape=[jax.ShapeDtypeStruct((N_HEADS_A, batch, seq, HEAD_DIM), BF16),
                   jax.ShapeDtypeStruct((batch, seq, LANES), F32)],
        scratch_shapes=[pltpu.VMEM((N_WINDOW_PLACEMENTS, QB_DIL, KW_DIL), F32)],
        compiler_params=_params("parallel", "arbitrary"),
        name=f"dilated_d{dil}" + ("_shifted" if shifted else ""),
    )(qkv_rm, qkv_rm, qkv_rm)


def _gqa_kernel(q_ref, k_ref, v_ref, o_ref, *, bk, shifted):
    rep, bq, dh = q_ref.shape
    seq = k_ref.shape[0]
    rows = rep * bq
    q = q_ref[...].reshape(rows, dh)

    def scores(c):
        c0 = pl.multiple_of(c * bk, bk)
        k = k_ref[pl.ds(c0, bk), :]
        v = v_ref[pl.ds(c0, bk), :]
        return lax.dot_general(q, k, (((1,), (1,)), ((), ())), preferred_element_type=F32), v

    if shifted:
        def body(c, carry):
            m, l, acc = carry
            s, v = scores(c)
            m_new = jnp.maximum(m, jnp.max(s, axis=-1, keepdims=True))
            alpha = jnp.exp2(m - m_new)
            p = jnp.exp2(s - m_new)
            l = alpha * l + jnp.sum(p, axis=-1, keepdims=True)
            acc = alpha * acc + jnp.dot(p.astype(BF16), v, preferred_element_type=F32)
            return m_new, l, acc

        init = (jnp.full((rows, 1), NEG, F32), jnp.zeros((rows, 1), F32), jnp.zeros((rows, dh), F32))
        _, l, acc = lax.fori_loop(0, seq // bk, body, init)
    else:
        def body(c, carry):
            lsum, acc = carry
            s, v = scores(c)
            p = jnp.exp2(s)
            for tile in range(bk // LANES):
                lsum = lsum + p[:, tile * LANES:(tile + 1) * LANES]
            acc = acc + jnp.dot(p.astype(BF16), v, preferred_element_type=F32)
            return lsum, acc

        init = (jnp.zeros((rows, LANES), F32), jnp.zeros((rows, dh), F32))
        lsum, acc = lax.fori_loop(0, seq // bk, body, init, unroll=4)
        l = jnp.sum(lsum, axis=-1, keepdims=True)
    o = acc / l
    for h in range(rep):
        o_ref[:, h * dh:(h + 1) * dh] = o[h * bq:(h + 1) * bq].astype(o_ref.dtype)


def _gqa(proj_b, batch, seq, shifted):
    t = batch * seq
    bq = _pick(seq, 256)
    bk = _pick(seq, 512)
    nq = seq // bq
    view = proj_b.reshape(N_COLS_B, batch, seq, HEAD_DIM)
    return pl.pallas_call(
        functools.partial(_gqa_kernel, bk=bk, shifted=shifted),
        grid=(batch, N_KV_B, nq),
        in_specs=[
            pl.BlockSpec((GQA_REP, None, bq, HEAD_DIM), lambda b, g, i: (g, b, i, 0)),
            pl.BlockSpec((None, None, seq, HEAD_DIM), lambda b, g, i: (N_HEADS_B + g, b, 0, 0)),
            pl.BlockSpec((None, None, seq, HEAD_DIM), lambda b, g, i: (N_HEADS_B + N_KV_B + g, b, 0, 0)),
        ],
        out_specs=pl.BlockSpec((bq, GQA_REP * HEAD_DIM), lambda b, g, i: (b * nq + i, g)),
        out_shape=jax.ShapeDtypeStruct((t, N_HEADS_B * HEAD_DIM), BF16),
        compiler_params=_params("parallel", "parallel", "arbitrary"),
        name="gqa" + ("_shifted" if shifted else ""),
    )(view, view, view)


def _outproj_kernel(o1_ref, o4_ref, o16_ref, l1_ref, l4_ref, l16_ref, ob_ref, ona_ref, onb_ref,
                    w_ref, x_ref, out_ref, mixed_ref, oa_ref, t4_ref, t16_ref):
    wa = N_HEADS_A * HEAD_DIM
    wb = ob_ref.shape[1]
    bm = oa_ref.shape[0]

    def to_token_order(tmp_ref, slot, load_residue, dil):
        for r in range(dil):
            tmp_ref[slot, pl.ds(r, bm // dil, stride=dil), :] = load_residue(r)
        return tmp_ref[slot]

    l1 = l1_ref[...]
    l4 = to_token_order(t4_ref, N_HEADS_A, lambda r: l4_ref[r], 4)
    l16 = to_token_order(t16_ref, N_HEADS_A, lambda r: l16_ref[r], 16)
    mx = jnp.maximum(jnp.maximum(l1, l4), l16)
    e1, e4, e16 = jnp.exp2(l1 - mx), jnp.exp2(l4 - mx), jnp.exp2(l16 - mx)
    den = e1 + e4 + e16
    w1, w4, w16 = e1 / den, e4 / den, e16 / den
    ss = jnp.zeros((bm, HEAD_DIM), F32)
    for h in range(N_HEADS_A):
        o4 = to_token_order(t4_ref, h, lambda r: o4_ref[h, r].astype(F32), 4)
        o16 = to_token_order(t16_ref, h, lambda r: o16_ref[h, r].astype(F32), 16)
        oa = (w1[:, h:h + 1] * o1_ref[h].astype(F32) + w4[:, h:h + 1] * o4
              + w16[:, h:h + 1] * o16)
        oa_ref[:, h * HEAD_DIM:(h + 1) * HEAD_DIM] = oa
        ss = ss + oa * oa
    ra = lax.rsqrt(jnp.sum(ss, axis=-1, keepdims=True) / wa + EPS)
    mixed_ref[:, :wa] = (oa_ref[...] * ra * ona_ref[...]).astype(BF16)
    ob = ob_ref[...].astype(F32)
    rb = lax.rsqrt(jnp.mean(ob * ob, axis=-1, keepdims=True) + EPS)
    mixed_ref[:, wa:wa + wb] = (ob * rb * onb_ref[...]).astype(BF16)
    out_ref[...] = x_ref[...] + jnp.dot(mixed_ref[...], w_ref[...], preferred_element_type=F32)


def _outproj(o_branches, lse_branches, ob, on_a, on_b, w_all, layer, x2, batch, seq):
    t, d = x2.shape
    bm = _pick(seq, 256)
    nps = seq // bm
    wa = N_HEADS_A * HEAD_DIM
    wb = ob.shape[1]
    o1, o4, o16 = o_branches
    l1, l4, l16 = lse_branches
    o1 = o1.reshape(N_HEADS_A, t, HEAD_DIM)
    l1 = l1.reshape(t, LANES)

    def o_spec(dil):
        return pl.BlockSpec((N_HEADS_A, None, dil, bm // dil, HEAD_DIM),
                            lambda i: (0, i // nps, 0, i % nps, 0))

    def l_spec(dil):
        return pl.BlockSpec((None, dil, bm // dil, LANES), lambda i: (i // nps, 0, i % nps, 0))

    def rm(a, dil):
        return a.reshape(a.shape[:-2] + (dil, seq // dil, a.shape[-1]))

    return pl.pallas_call(
        _outproj_kernel,
        grid=(t // bm,),
        in_specs=[
            pl.BlockSpec((N_HEADS_A, bm, HEAD_DIM), lambda i: (0, i, 0)), o_spec(4), o_spec(16),
            pl.BlockSpec((bm, LANES), lambda i: (i, 0)), l_spec(4), l_spec(16),
            pl.BlockSpec((bm, wb), lambda i: (i, 0)),
            pl.BlockSpec((1, wa), lambda i: (0, 0)),
            pl.BlockSpec((1, wb), lambda i: (0, 0)),
            pl.BlockSpec((None, wa + wb, d), lambda i: (layer, 0, 0), pipeline_mode=pl.Buffered(1)),
            pl.BlockSpec((bm, d), lambda i: (i, 0)),
        ],
        out_specs=pl.BlockSpec((bm, d), lambda i: (i, 0)),
        out_shape=jax.ShapeDtypeStruct((t, d), F32),
        scratch_shapes=[pltpu.VMEM((bm, wa + wb), BF16), pltpu.VMEM((bm, wa), F32),
                        pltpu.VMEM((N_HEADS_A + 1, bm, HEAD_DIM), F32),
                        pltpu.VMEM((N_HEADS_A + 1, bm, HEAD_DIM), F32)],
        compiler_params=_params("parallel"),
        name="outproj",
    )(o1, rm(o4, 4), rm(o16, 16), l1, rm(l4, 4), rm(l16, 16), ob, on_a, on_b, w_all, x2)


def _swiglu_step(h, wg_ref, wu_ref, wd_ref, acc_ref):
    g = jnp.dot(h, wg_ref[...], preferred_element_type=F32)
    u = jnp.dot(h, wu_ref[...], preferred_element_type=F32)
    a = (g * jax.nn.sigmoid(g) * u).astype(BF16)
    acc_ref[...] += jnp.dot(a, wd_ref[...], preferred_element_type=F32)


CAST_COLS = 512


def _cast_chunks(n_steps, f, rows_gu, rows_d):
    ncb = f // CAST_COLS
    assert f % CAST_COLS == 0 and n_steps % ncb == 0
    nrg = n_steps // ncb
    assert rows_gu % nrg == 0 and rows_d % n_steps == 0
    rpc, rpd = rows_gu // nrg, rows_d // n_steps
    assert rpc % 16 == 0 and rpd % 16 == 0
    return ncb, rpc, rpd


def _dense_ffn_kernel(x_ref, g_ref, wg_ref, wu_ref, wd_ref, eg_ref, eu_ref, ed_ref,
                      out_ref, og_ref, ou_ref, od_ref,
                      hn_ref, acc_ref, ing_ref, inu_ref, ind_ref, outg_ref, outu_ref, outd_ref,
                      sem_in, sem_out, *, li, ncb, rpc, rpd):
    i = pl.program_id(0)
    j = pl.program_id(1)
    nj = pl.num_programs(1)
    step = i * nj + j
    last = pl.num_programs(0) * nj - 1
    slot = step % 2

    def in_copies(s, sl):
        rg = s // ncb
        cb = s - rg * ncb
        r0 = pl.multiple_of(rg * rpc, 16)
        c0 = pl.multiple_of(cb * CAST_COLS, CAST_COLS)
        d0 = pl.multiple_of(s * rpd, 16)
        return (pltpu.make_async_copy(eg_ref.at[li, pl.ds(r0, rpc), pl.ds(c0, CAST_COLS)],
                                      ing_ref.at[sl], sem_in.at[sl, 0]),
                pltpu.make_async_copy(eu_ref.at[li, pl.ds(r0, rpc), pl.ds(c0, CAST_COLS)],
                                      inu_ref.at[sl], sem_in.at[sl, 1]),
                pltpu.make_async_copy(ed_ref.at[li, pl.ds(d0, rpd), :], ind_ref.at[sl], sem_in.at[sl, 2]))

    def out_copies(s, sl):
        rg = s // ncb
        cb = s - rg * ncb
        r0 = pl.multiple_of(rg * rpc, 16)
        c0 = pl.multiple_of(cb * CAST_COLS, CAST_COLS)
        d0 = pl.multiple_of(s * rpd, 16)
        return (pltpu.make_async_copy(outg_ref.at[sl], og_ref.at[pl.ds(r0, rpc), pl.ds(c0, CAST_COLS)],
                                      sem_out.at[sl, 0]),
                pltpu.make_async_copy(outu_ref.at[sl], ou_ref.at[pl.ds(r0, rpc), pl.ds(c0, CAST_COLS)],
                                      sem_out.at[sl, 1]),
                pltpu.make_async_copy(outd_ref.at[sl], od_ref.at[pl.ds(d0, rpd), :], sem_out.at[sl, 2]))

    @pl.when(step == 0)
    def _():
        for c in in_copies(step, slot):
            c.start()

    @pl.when(step >= 2)
    def _():
        for c in out_copies(step - 2, slot):
            c.wait()

    @pl.when(j == 0)
    def _():
        _rms_to_bf16(x_ref, g_ref, hn_ref)
        acc_ref[...] = jnp.zeros_like(acc_ref)

    for c in in_copies(step, slot):
        c.wait()
    outg_ref[slot] = ing_ref[slot].astype(BF16)
    outu_ref[slot] = inu_ref[slot].astype(BF16)
    outd_ref[slot] = ind_ref[slot].astype(BF16)
    _swiglu_step(hn_ref[...], wg_ref, wu_ref, wd_ref, acc_ref)
    for c in out_copies(step, slot):
        c.start()

    @pl.when(step < last)
    def _():
        for c in in_copies(step + 1, 1 - slot):
            c.start()

    @pl.when(j == nj - 1)
    def _():
        out_ref[...] = x_ref[...] + acc_ref[...]

    @pl.when(step == last)
    def _():
        for c in out_copies(step, slot):
            c.wait()

        @pl.when(step >= 1)
        def _():
            for c in out_copies(step - 1, 1 - slot):
                c.wait()


def _dense_ffn(x2, g, wg, wu, wd, li, eg, eu, ed):
    t, d = x2.shape
    f = wg.shape[2]
    n_layers, n_exp, de, fe = eg.shape
    bm = _pick(t, 1024)
    bf = _pick(f, 256)
    grid = (t // bm, f // bf)
    rows_gu, rows_d = n_exp * de, n_exp * fe
    ncb, rpc, rpd = _cast_chunks(grid[0] * grid[1], fe, rows_gu, rows_d)
    single = pl.Buffered(1)
    any_spec = pl.BlockSpec(memory_space=pl.ANY)
    out, og, ou, od = pl.pallas_call(
        functools.partial(_dense_ffn_kernel, li=li, ncb=ncb, rpc=rpc, rpd=rpd),
        grid=grid,
        in_specs=[
            pl.BlockSpec((bm, d), lambda i, j: (i, 0), pipeline_mode=single),
            pl.BlockSpec((1, d), lambda i, j: (0, 0)),
            pl.BlockSpec((None, d, bf), lambda i, j: (li, 0, j)),
            pl.BlockSpec((None, d, bf), lambda i, j: (li, 0, j)),
            pl.BlockSpec((None, bf, d), lambda i, j: (li, j, 0)),
            any_spec, any_spec, any_spec,
        ],
        out_specs=[pl.BlockSpec((bm, d), lambda i, j: (i, 0), pipeline_mode=single),
                   any_spec, any_spec, any_spec],
        out_shape=[jax.ShapeDtypeStruct((t, d), F32),
                   jax.ShapeDtypeStruct((rows_gu, fe), BF16), jax.ShapeDtypeStruct((rows_gu, fe), BF16),
                   jax.ShapeDtypeStruct((rows_d, de), BF16)],
        scratch_shapes=[pltpu.VMEM((bm, d), BF16), pltpu.VMEM((bm, d), F32),
                        pltpu.VMEM((2, rpc, CAST_COLS), F32), pltpu.VMEM((2, rpc, CAST_COLS), F32),
                        pltpu.VMEM((2, rpd, de), F32),
                        pltpu.VMEM((2, rpc, CAST_COLS), BF16), pltpu.VMEM((2, rpc, CAST_COLS), BF16),
                        pltpu.VMEM((2, rpd, de), BF16),
                        pltpu.SemaphoreType.DMA((2, 3)), pltpu.SemaphoreType.DMA((2, 3))],
        compiler_params=_params("arbitrary", "arbitrary"),
        name="dense_ffn",
    )(x2, g, wg, wu, wd, eg.reshape(n_layers, rows_gu, fe), eu.reshape(n_layers, rows_gu, fe),
      ed.reshape(n_layers, rows_d, de))
    return (out, og.reshape(1, n_exp, de, fe), ou.reshape(1, n_exp, de, fe),
            od.reshape(1, n_exp, fe, de))


INFO_E1, INFO_E2, INFO_R1, INFO_R2, INFO_G1, INFO_G2 = range(6)


def _router_kernel(x_ref, g_ref, wr_ref, hn_ref, info_ref, cnt_ref, run_ref):
    i = pl.program_id(0)
    bm = x_ref.shape[0]

    @pl.when(i == 0)
    def _():
        run_ref[...] = jnp.zeros_like(run_ref)

    x = x_ref[...]
    ms = jnp.mean(x * x, axis=-1, keepdims=True)
    hn = x * lax.rsqrt(ms + EPS) * g_ref[...]
    hn_ref[...] = hn
    logits = jnp.dot(hn, wr_ref[...], preferred_element_type=F32, precision=lax.Precision.HIGHEST)
    lane = lax.broadcasted_iota(jnp.int32, (bm, LANES), 1)
    lg = jnp.where(lane < N_EXPERTS, logits, -jnp.inf)
    v1 = jnp.max(lg, axis=-1, keepdims=True)
    i1 = jnp.min(jnp.where(lg == v1, lane, LANES), axis=-1, keepdims=True)
    lg2 = jnp.where(lane == i1, -jnp.inf, lg)
    v2 = jnp.max(lg2, axis=-1, keepdims=True)
    i2 = jnp.min(jnp.where(lg2 == v2, lane, LANES), axis=-1, keepdims=True)
    e = jnp.exp(v2 - v1)
    g1 = 1.0 / (1.0 + e)
    g2 = e / (1.0 + e)
    hot1 = lane == i1
    hot2 = lane == i2
    onehot = (hot1 | hot2).astype(F32)
    tri = (lax.broadcasted_iota(jnp.int32, (bm, bm), 1)
           < lax.broadcasted_iota(jnp.int32, (bm, bm), 0)).astype(BF16)
    rank = jnp.dot(tri, onehot.astype(BF16), preferred_element_type=F32) + run_ref[0:1, :]
    r1 = jnp.sum(jnp.where(hot1, rank, 0.0), axis=-1, keepdims=True)
    r2 = jnp.sum(jnp.where(hot2, rank, 0.0), axis=-1, keepdims=True)
    run_ref[...] = run_ref[...] + jnp.sum(onehot, axis=0, keepdims=True)
    info = jnp.zeros((bm, LANES), F32)
    for col, val in ((INFO_E1, i1.astype(F32)), (INFO_E2, i2.astype(F32)), (INFO_R1, r1),
                     (INFO_R2, r2), (INFO_G1, g1), (INFO_G2, g2)):
        info = jnp.where(lane == col, val, info)
    info_ref[...] = info
    cnt_ref[...] = run_ref[...]


def _router(x2, g, wr_pad):
    t, d = x2.shape
    bm = _pick(t, 512)
    return pl.pallas_call(
        _router_kernel,
        grid=(t // bm,),
        in_specs=[
            pl.BlockSpec((bm, d), lambda i: (i, 0)),
            pl.BlockSpec((1, d), lambda i: (0, 0)),
            pl.BlockSpec((d, LANES), lambda i: (0, 0)),
        ],
        out_specs=[
            pl.BlockSpec((bm, d), lambda i: (i, 0)),
            pl.BlockSpec((bm, LANES), lambda i: (i, 0)),
            pl.BlockSpec((8, LANES), lambda i: (0, 0)),
        ],
        out_shape=[jax.ShapeDtypeStruct((t, d), F32), jax.ShapeDtypeStruct((t, LANES), F32),
                   jax.ShapeDtypeStruct((8, LANES), F32)],
        scratch_shapes=[pltpu.VMEM((8, LANES), F32)],
        compiler_params=_params("arbitrary"),
        name="router",
    )(x2, g, wr_pad)


def _scatter_kernel(pos1_ref, pos2_ref, hn_ref, hs_in_ref, hs_ref, sem):
    del hs_in_ref
    bt = hn_ref.shape[0]

    def copies(k):
        src = hn_ref.at[pl.ds(k, 1), :]
        return (pltpu.make_async_copy(src, hs_ref.at[pl.ds(pos1_ref[0, 0, k], 1), :], sem.at[0]),
                pltpu.make_async_copy(src, hs_ref.at[pl.ds(pos2_ref[0, 0, k], 1), :], sem.at[1]))

    def start(k, c):
        a, b = copies(k)
        a.start()
        b.start()
        return c

    def wait(k, c):
        a, b = copies(k)
        a.wait()
        b.wait()
        return c

    lax.fori_loop(0, bt, start, 0, unroll=DMA_UNROLL)
    lax.fori_loop(0, bt, wait, 0, unroll=DMA_UNROLL)


def _scatter_rows(hn, pos1, pos2, n_rows):
    t, d = hn.shape
    bt = _pick(t, 512)
    smem_spec = pl.BlockSpec((1, 1, bt), lambda i: (i, 0, 0), memory_space=pltpu.SMEM)
    any_spec = pl.BlockSpec(memory_space=pl.ANY)
    return pl.pallas_call(
        _scatter_kernel,
        grid=(t // bt,),
        in_specs=[smem_spec, smem_spec, pl.BlockSpec((bt, d), lambda i: (i, 0)), any_spec],
        out_specs=any_spec,
        out_shape=jax.ShapeDtypeStruct((n_rows, d), hn.dtype),
        scratch_shapes=[pltpu.SemaphoreType.DMA((2,))],
        input_output_aliases={3: 0},
        compiler_params=_params("arbitrary"),
        name="scatter_rows",
    )(pos1.reshape(t // bt, 1, bt), pos2.reshape(t // bt, 1, bt), hn, jnp.zeros((n_rows, d), hn.dtype))


def _grouped_ffn_kernel(be_ref, na_ref, hs_ref, wg_ref, wu_ref, wd_ref, out_ref, hb_ref, acc_ref):
    del be_ref
    i = pl.program_id(0)
    j = pl.program_id(1)
    active = i < na_ref[0]

    @pl.when(active & (j == 0))
    def _():
        hb_ref[...] = hs_ref[...].astype(BF16)
        acc_ref[...] = jnp.zeros_like(acc_ref)

    @pl.when(active)
    def _():
        _swiglu_step(hb_ref[...], wg_ref, wu_ref, wd_ref, acc_ref)

    @pl.when(j == pl.num_programs(1) - 1)
    def _():
        @pl.when(active)
        def _():
            out_ref[...] = acc_ref[...]

        @pl.when(jnp.logical_not(active))
        def _():
            out_ref[...] = jnp.zeros_like(out_ref)


def _grouped_ffn(hs, block_expert, n_active, wg, wu, wd, li, bm):
    p, d = hs.shape
    f = wg.shape[3]
    bf = _pick(f, 512)
    nf = f // bf

    def jj(i, j, na):
        return jnp.where(i < na[0], j, nf - 1)

    grid_spec = pltpu.PrefetchScalarGridSpec(
        num_scalar_prefetch=2,
        grid=(p // bm, nf),
        in_specs=[
            pl.BlockSpec((bm, d), lambda i, j, be, na: (jnp.maximum(jnp.minimum(i, na[0] - 1), 0), 0)),
            pl.BlockSpec((None, None, d, bf), lambda i, j, be, na: (li, be[i], 0, jj(i, j, na))),
            pl.BlockSpec((None, None, d, bf), lambda i, j, be, na: (li, be[i], 0, jj(i, j, na))),
            pl.BlockSpec((None, None, bf, d), lambda i, j, be, na: (li, be[i], jj(i, j, na), 0)),
        ],
        out_specs=pl.BlockSpec((bm, d), lambda i, j, be, na: (i, 0)),
        scratch_shapes=[pltpu.VMEM((bm, d), BF16), pltpu.VMEM((bm, d), F32)],
    )
    return pl.pallas_call(
        _grouped_ffn_kernel,
        grid_spec=grid_spec,
        out_shape=jax.ShapeDtypeStruct((p, d), F32),
        compiler_params=_params("arbitrary", "arbitrary"),
        name="grouped_ffn",
    )(block_expert, n_active, hs, wg, wu, wd)


def _combine_kernel(pos1_ref, pos2_ref, x_ref, info_ref, y_ref, out_ref, buf1_ref, buf2_ref, sem):
    bt = x_ref.shape[0]

    def copies(k):
        return (pltpu.make_async_copy(y_ref.at[pl.ds(pos1_ref[0, 0, k], 1), :],
                                      buf1_ref.at[pl.ds(k, 1), :], sem.at[0]),
                pltpu.make_async_copy(y_ref.at[pl.ds(pos2_ref[0, 0, k], 1), :],
                                      buf2_ref.at[pl.ds(k, 1), :], sem.at[1]))

    def start(k, c):
        a, b = copies(k)
        a.start()
        b.start()
        return c

    def wait(k, c):
        a, b = copies(k)
        a.wait()
        b.wait()
        return c

    lax.fori_loop(0, bt, start, 0, unroll=DMA_UNROLL)
    lax.fori_loop(0, bt, wait, 0, unroll=DMA_UNROLL)
    info = info_ref[...]
    g1 = info[:, INFO_G1:INFO_G1 + 1]
    g2 = info[:, INFO_G2:INFO_G2 + 1]
    out_ref[...] = x_ref[...] + g1 * buf1_ref[...] + g2 * buf2_ref[...]


def _combine(x2, info, y, pos1, pos2):
    t, d = x2.shape
    bt = _pick(t, 256)
    smem_spec = pl.BlockSpec((1, 1, bt), lambda i: (i, 0, 0), memory_space=pltpu.SMEM)
    return pl.pallas_call(
        _combine_kernel,
        grid=(t // bt,),
        in_specs=[
            smem_spec, smem_spec,
            pl.BlockSpec((bt, d), lambda i: (i, 0)),
            pl.BlockSpec((bt, LANES), lambda i: (i, 0)),
            pl.BlockSpec(memory_space=pl.ANY),
        ],
        out_specs=pl.BlockSpec((bt, d), lambda i: (i, 0)),
        out_shape=jax.ShapeDtypeStruct((t, d), F32),
        scratch_shapes=[pltpu.VMEM((bt, d), F32), pltpu.VMEM((bt, d), F32),
                        pltpu.SemaphoreType.DMA((2,))],
        compiler_params=_params("arbitrary"),
        name="combine",
    )(pos1.reshape(t // bt, 1, bt), pos2.reshape(t // bt, 1, bt), x2, info, y)


MOE_BLOCK_ROWS = 512


def _moe_ffn(x2, g, wr, wg, wu, wd, li):
    t, d = x2.shape
    bm = MOE_BLOCK_ROWS
    wr_pad = jnp.zeros((d, LANES), F32).at[:, :N_EXPERTS].set(wr)
    hn, info, cnt = _router(x2, g, wr_pad)
    e1 = info[:, INFO_E1].astype(jnp.int32)
    e2 = info[:, INFO_E2].astype(jnp.int32)
    r1 = info[:, INFO_R1].astype(jnp.int32)
    r2 = info[:, INFO_R2].astype(jnp.int32)
    counts = cnt[0, :N_EXPERTS].astype(jnp.int32)
    blocks_per_expert = (counts + bm - 1) // bm
    block_end = jnp.cumsum(blocks_per_expert)
    offsets = (block_end - blocks_per_expert) * bm
    pos1 = offsets[e1] + r1
    pos2 = offsets[e2] + r2
    n_blocks = (2 * t) // bm + N_EXPERTS
    n_active = block_end[-1:]
    ids = jnp.minimum(jnp.arange(n_blocks, dtype=jnp.int32), n_active[0] - 1)
    block_expert = jnp.sum((ids[:, None] >= block_end[None, :]).astype(jnp.int32), axis=1)
    block_expert = jnp.minimum(block_expert, N_EXPERTS - 1)
    hs = _scatter_rows(hn, pos1, pos2, n_blocks * bm)
    y = _grouped_ffn(hs, block_expert, n_active.astype(jnp.int32), wg, wu, wd, li, bm)
    return _combine(x2, info, y, pos1, pos2)


def _attention(proj_a, proj_b, batch, seq, shifted):
    views = [a.reshape(N_COLS_A, batch, seq, HEAD_DIM) for a in proj_a]
    branches = [_dilated_branch(v, batch, seq, dil, shifted) for v, dil in zip(views, DILATIONS)]
    ob = _gqa(proj_b, batch, seq, shifted)
    return [b[0] for b in branches], [b[1] for b in branches], ob


def kernel(x, g_mix, w_in, qn_a, kn_a, qn_b, kn_b, on_a, on_b, w_out, g_ffn, w_gate, w_up, w_down,
           w_router, we_gate, we_up, we_down):
    batch, seq, d = x.shape
    depth = w_in.shape[0]
    t = batch * seq
    assert all(w // (2 * dil) == BAND_HALF for w, dil in DILATED_PATTERNS) and DILATIONS == (1, 4, 16)
    assert w_in.shape[2] == (N_COLS_A + N_COLS_B) * HEAD_DIM
    scale = HEAD_DIM ** -0.5 * LOG2_E
    tables = _rope_tables(seq)
    w_in_b, w_out_b = w_in.astype(BF16), w_out.astype(BF16)
    w_gate_b, w_up_b, w_down_b = w_gate.astype(BF16), w_up.astype(BF16), w_down.astype(BF16)
    assert depth % 2 == 0
    x2 = x.reshape(t, d)
    for l in range(depth):
        gains = jnp.concatenate([jnp.tile(qn_a[l] * scale, N_HEADS_A), jnp.tile(kn_a[l], N_HEADS_A),
                                 jnp.ones((N_HEADS_A * HEAD_DIM,), F32),
                                 jnp.tile(qn_b[l] * scale, N_HEADS_B), jnp.tile(kn_b[l], N_KV_B),
                                 jnp.ones((N_KV_B * HEAD_DIM,), F32)])[None, :]
        bound = 1.02 * HEAD_DIM ** 0.5 * jnp.maximum(
            jnp.max(jnp.abs(qn_a[l])) * jnp.max(jnp.abs(kn_a[l])),
            jnp.max(jnp.abs(qn_b[l])) * jnp.max(jnp.abs(kn_b[l])))
        a1, a4, a16, proj_b = _inproj(x2, g_mix[l][None, :], w_in_b, l, gains,
                                      tables[0] + tables[1], batch, seq)
        attend = functools.partial(_attention, (a1, a4, a16), proj_b, batch, seq)
        o_br, lse_br, ob = lax.cond(bound <= MAX_UNSHIFTED_SCORE, lambda: attend(False),
                                    lambda: attend(True))
        x2 = _outproj(o_br, lse_br, ob, on_a[l][None, :], on_b[l][None, :], w_out_b, l, x2, batch, seq)
        i = l // 2
        if l % 2 == 0:
            x2, *experts_b = _dense_ffn(x2, g_ffn[l][None, :], w_gate_b, w_up_b, w_down_b, i,
                                        we_gate, we_up, we_down)
        else:
            x2 = _moe_ffn(x2, g_ffn[l][None, :], w_router[i], *experts_b, 0)
    return x2.reshape(batch, seq, d)
```

```python
import functools

import jax
import jax.numpy as jnp
from jax import lax
from jax.experimental import pallas as pl
from jax.experimental.pallas import tpu as pltpu

HEAD_DIM = 128
N_HEADS_A = 8
N_HEADS_B = 8
N_KV_B = 2
GQA_REP = N_HEADS_B // N_KV_B
N_COLS_A = 3 * N_HEADS_A
N_COLS_B = N_HEADS_B + 2 * N_KV_B
DILATED_PATTERNS = ((128, 1), (512, 4), (2048, 16))
DILATIONS = tuple(d for _, d in DILATED_PATTERNS)
BAND_HALF = 64
GRID_W = 64
ROPE_THETA = 10000.0
N_EXPERTS = 8
EPS = 1e-6
NEG = -1e30
MAX_UNSHIFTED_SCORE = 40.0
LOG2_E = 1.4426950408889634

HEADS_PER_STEP = 4
LANES = 128
DMA_UNROLL = 8
V7X_VMEM_LIMIT_BYTES = 56 * 1024 * 1024

BF16 = jnp.bfloat16
F32 = jnp.float32


def _params(*semantics):
    return pltpu.CompilerParams(dimension_semantics=semantics, vmem_limit_bytes=V7X_VMEM_LIMIT_BYTES)


def _pick(n, pref):
    b = min(n, pref)
    assert n % b == 0, (n, pref)
    return b


def _rope_angles(pos, dim):
    inv = ROPE_THETA ** (-jnp.arange(0, dim, 2, dtype=F32) / dim)
    ang = pos.astype(F32)[:, None] * inv[None, :]
    return jnp.cos(ang), jnp.sin(ang)


def _rope_tables(seq):
    pos = jnp.arange(seq)
    c1, s1 = _rope_angles(pos, HEAD_DIM)
    cos_a = jnp.concatenate([c1, c1], axis=-1)
    sin_a = jnp.concatenate([-s1, s1], axis=-1)
    cr, sr = _rope_angles(pos // GRID_W, HEAD_DIM // 2)
    cc, sc = _rope_angles(pos % GRID_W, HEAD_DIM // 2)
    z = jnp.zeros_like(sr)
    cos_b = jnp.concatenate([cr, cr, cc, cc], axis=-1)
    sin_b_up = jnp.concatenate([-sr, z, -sc, z], axis=-1)
    sin_b_dn = jnp.concatenate([z, sr, z, sc], axis=-1)
    return (cos_a, sin_a), (cos_b, sin_b_up, sin_b_dn)


def _rms_to_bf16(x_ref, g_ref, hn_ref):
    x = x_ref[...]
    ms = jnp.mean(x * x, axis=-1, keepdims=True)
    hn_ref[...] = (x * lax.rsqrt(ms + EPS) * g_ref[...]).astype(BF16)


def _head(y, hh):
    return y[:, hh * HEAD_DIM:(hh + 1) * HEAD_DIM]


def _inproj_kernel(x_ref, g_ref, w_ref, gain_ref, cosa_ref, sina_ref, cosb_ref, sinbu_ref, sinbd_ref,
                   a1_ref, a4_ref, a16_ref, b_ref, hn_ref, tmp_ref):
    bm = x_ref.shape[0]
    _rms_to_bf16(x_ref, g_ref, hn_ref)
    width = HEADS_PER_STEP * HEAD_DIM
    for c in range((N_COLS_A + N_COLS_B) // HEADS_PER_STEP):
        y = jnp.dot(hn_ref[...], w_ref[:, c * width:(c + 1) * width], preferred_element_type=F32)
        for hh in range(HEADS_PER_STEP):
            col = c * HEADS_PER_STEP + hh
            gain = gain_ref[:, col * HEAD_DIM:(col + 1) * HEAD_DIM]
            yh = _head(y, hh)
            is_a = col < N_COLS_A
            is_v = (2 * N_HEADS_A <= col < N_COLS_A) or col >= N_COLS_A + N_HEADS_B + N_KV_B
            if is_v:
                res = yh
            else:
                ms = jnp.mean(yh * yh, axis=-1, keepdims=True)
                yn = yh * lax.rsqrt(ms + EPS) * gain
                if is_a:
                    res = yn * cosa_ref[...] + pltpu.roll(yn, 64, 1) * sina_ref[...]
                else:
                    res = (yn * cosb_ref[...] + pltpu.roll(yn, 96, 1) * sinbu_ref[...]
                           + pltpu.roll(yn, 32, 1) * sinbd_ref[...])
            if is_a:
                a1_ref[col] = res.astype(BF16)
                tmp_ref[col] = res
                for out_ref, dil in ((a4_ref, 4), (a16_ref, 16)):
                    for r in range(dil):
                        out_ref[col, r] = tmp_ref[col, pl.ds(r, bm // dil, stride=dil), :].astype(BF16)
            else:
                b_ref[col - N_COLS_A] = res.astype(BF16)


def _inproj(x2, g, w_all, layer, gain_cols, tables, batch, seq):
    t, d = x2.shape
    bm = _pick(seq, 256)
    nps = seq // bm
    n = w_all.shape[2]
    tab_spec = pl.BlockSpec((bm, HEAD_DIM), lambda i: (i % nps, 0))

    def rm_spec(dil):
        return pl.BlockSpec((N_COLS_A, None, dil, bm // dil, HEAD_DIM),
                            lambda i: (0, i // nps, 0, i % nps, 0))

    def rm_shape(dil):
        return jax.ShapeDtypeStruct((N_COLS_A, batch, dil, seq // dil, HEAD_DIM), BF16)

    return pl.pallas_call(
        _inproj_kernel,
        grid=(t // bm,),
        in_specs=[
            pl.BlockSpec((bm, d), lambda i: (i, 0)),
            pl.BlockSpec((1, d), lambda i: (0, 0)),
            pl.BlockSpec((None, d, n), lambda i: (layer, 0, 0), pipeline_mode=pl.Buffered(1)),
            pl.BlockSpec((1, n), lambda i: (0, 0)),
            tab_spec, tab_spec, tab_spec, tab_spec, tab_spec,
        ],
        out_specs=[pl.BlockSpec((N_COLS_A, bm, HEAD_DIM), lambda i: (0, i, 0)), rm_spec(4), rm_spec(16),
                   pl.BlockSpec((N_COLS_B, bm, HEAD_DIM), lambda i: (0, i, 0))],
        out_shape=[jax.ShapeDtypeStruct((N_COLS_A, t, HEAD_DIM), BF16), rm_shape(4), rm_shape(16),
                   jax.ShapeDtypeStruct((N_COLS_B, t, HEAD_DIM), BF16)],
        scratch_shapes=[pltpu.VMEM((bm, d), BF16), pltpu.VMEM((N_COLS_A, bm, HEAD_DIM), F32)],
        compiler_params=_params("parallel"),
        name="inproj",
    )(x2, g, w_all, gain_cols, *tables)


QB_DIL = 128
KW_DIL = QB_DIL + 2 * BAND_HALF
BAND_SHIFT = BAND_HALF.bit_length() - 1
assert 1 << BAND_SHIFT == BAND_HALF
N_WINDOW_PLACEMENTS = QB_DIL // BAND_HALF + 1
DIL_UNROLL = 8


def _dilated_kernel(q_ref, k_ref, v_ref, o_ref, lse_ref, bias_ref, *, seg, shifted):
    h = pl.program_id(1)
    rows = q_ref.shape[0]
    lane = lax.broadcasted_iota(jnp.int32, (QB_DIL, LANES), 1)

    @pl.when(h == 0)
    def _():
        lse_ref[...] = jnp.zeros_like(lse_ref)
        rel0 = (lax.broadcasted_iota(jnp.int32, (QB_DIL, KW_DIL), 1)
                - lax.broadcasted_iota(jnp.int32, (QB_DIL, KW_DIL), 0))
        for c in range(bias_ref.shape[0]):
            bias_ref[c] = jnp.where(jnp.abs(rel0 - BAND_HALF * c) <= BAND_HALF, 0.0, NEG)

    def body(blk, carry):
        l0 = pl.multiple_of(blk * QB_DIL, QB_DIL)
        seg0 = l0 & (-seg)
        w0 = pl.multiple_of(jnp.clip(l0 - BAND_HALF, seg0, seg0 + seg - KW_DIL), BAND_HALF)
        q = q_ref[pl.ds(l0, QB_DIL), :]
        k = k_ref[pl.ds(w0, KW_DIL), :]
        v = v_ref[pl.ds(w0, KW_DIL), :]
        s = lax.dot_general(q, k, (((1,), (1,)), ((), ())), preferred_element_type=F32)
        s = s + bias_ref[lax.shift_right_logical(l0 - w0, BAND_SHIFT)]
        if shifted:
            m = jnp.max(s, axis=-1, keepdims=True)
            p = jnp.exp2(s - m)
        else:
            p = jnp.exp2(s)
        den = jnp.sum(p, axis=-1, keepdims=True)
        o = jnp.dot(p.astype(BF16), v, preferred_element_type=F32) / den
        lse = jnp.log2(den) + m if shifted else jnp.log2(den)
        o_ref[pl.ds(l0, QB_DIL), :] = o.astype(BF16)
        lse_ref[pl.ds(l0, QB_DIL), :] = jnp.where(lane == h, lse, lse_ref[pl.ds(l0, QB_DIL), :])
        return carry

    lax.fori_loop(0, rows // QB_DIL, body, 0, unroll=DIL_UNROLL)


def _dilated_branch(qkv_rm, batch, seq, dil, shifted):
    seg = seq // dil
    assert seg >= KW_DIL and seg % QB_DIL == 0 and seg & (seg - 1) == 0

    def spec(off):
        return pl.BlockSpec((None, None, seq, HEAD_DIM), lambda b, h: (off + h, b, 0, 0))

    return pl.pallas_call(
        functools.partial(_dilated_kernel, seg=seg, shifted=shifted),
        grid=(batch, N_HEADS_A),
        in_specs=[spec(0), spec(N_HEADS_A), spec(2 * N_HEADS_A)],
        out_specs=[pl.BlockSpec((None, None, seq, HEAD_DIM), lambda b, h: (h, b, 0, 0)),
                   pl.BlockSpec((None, seq, LANES), lambda b, h: (b, 0, 0))],
        out_shape=[jax.ShapeDtypeStruct((N_HEADS_A, batch, seq, HEAD_DIM), BF16),
                   jax.ShapeDtypeStruct((batch, seq, LANES), F32)],
        scratch_shapes=[pltpu.VMEM((N_WINDOW_PLACEMENTS, QB_DIL, KW_DIL), F32)],
        compiler_params=_params("parallel", "arbitrary"),
        name=f"dilated_d{dil}" + ("_shifted" if shifted else ""),
    )(qkv_rm, qkv_rm, qkv_rm)


def _gqa_kernel(q_ref, k_ref, v_ref, o_ref, *, bk, shifted):
    rep, bq, dh = q_ref.shape
    seq = k_ref.shape[0]
    rows = rep * bq
    q = q_ref[...].reshape(rows, dh)

    def scores(c):
        c0 = pl.multiple_of(c * bk, bk)
        k = k_ref[pl.ds(c0, bk), :]
        v = v_ref[pl.ds(c0, bk), :]
        return lax.dot_general(q, k, (((1,), (1,)), ((), ())), preferred_element_type=F32), v

    if shifted:
        def body(c, carry):
            m, l, acc = carry
            s, v = scores(c)
            m_new = jnp.maximum(m, jnp.max(s, axis=-1, keepdims=True))
            alpha = jnp.exp2(m - m_new)
            p = jnp.exp2(s - m_new)
            l = alpha * l + jnp.sum(p, axis=-1, keepdims=True)
            acc = alpha * acc + jnp.dot(p.astype(BF16), v, preferred_element_type=F32)
            return m_new, l, acc

        init = (jnp.full((rows, 1), NEG, F32), jnp.zeros((rows, 1), F32), jnp.zeros((rows, dh), F32))
        _, l, acc = lax.fori_loop(0, seq // bk, body, init)
    else:
        def body(c, carry):
            lsum, acc = carry
            s, v = scores(c)
            p = jnp.exp2(s)
            for tile in range(bk // LANES):
                lsum = lsum + p[:, tile * LANES:(tile + 1) * LANES]
            acc = acc + jnp.dot(p.astype(BF16), v, preferred_element_type=F32)
            return lsum, acc

        init = (jnp.zeros((rows, LANES), F32), jnp.zeros((rows, dh), F32))
        lsum, acc = lax.fori_loop(0, seq // bk, body, init, unroll=4)
        l = jnp.sum(lsum, axis=-1, keepdims=True)
    o = acc / l
    for h in range(rep):
        o_ref[:, h * dh:(h + 1) * dh] = o[h * bq:(h + 1) * bq].astype(o_ref.dtype)


def _gqa(proj_b, batch, seq, shifted):
    t = batch * seq
    bq = _pick(seq, 256)
    bk = _pick(seq, 512)
    nq = seq // bq
    view = proj_b.reshape(N_COLS_B, batch, seq, HEAD_DIM)
    return pl.pallas_call(
        functools.partial(_gqa_kernel, bk=bk, shifted=shifted),
        grid=(batch, N_KV_B, nq),
        in_specs=[
            pl.BlockSpec((GQA_REP, None, bq, HEAD_DIM), lambda b, g, i: (g, b, i, 0)),
            pl.BlockSpec((None, None, seq, HEAD_DIM), lambda b, g, i: (N_HEADS_B + g, b, 0, 0)),
            pl.BlockSpec((None, None, seq, HEAD_DIM), lambda b, g, i: (N_HEADS_B + N_KV_B + g, b, 0, 0)),
        ],
        out_specs=pl.BlockSpec((bq, GQA_REP * HEAD_DIM), lambda b, g, i: (b * nq + i, g)),
        out_shape=jax.ShapeDtypeStruct((t, N_HEADS_B * HEAD_DIM), BF16),
        compiler_params=_params("parallel", "parallel", "arbitrary"),
        name="gqa" + ("_shifted" if shifted else ""),
    )(view, view, view)


def _outproj_kernel(o1_ref, o4_ref, o16_ref, l1_ref, l4_ref, l16_ref, ob_ref, ona_ref, onb_ref,
                    w_ref, x_ref, out_ref, mixed_ref, oa_ref, t4_ref, t16_ref):
    wa = N_HEADS_A * HEAD_DIM
    wb = ob_ref.shape[1]
    bm = oa_ref.shape[0]

    def to_token_order(tmp_ref, slot, load_residue, dil):
        for r in range(dil):
            tmp_ref[slot, pl.ds(r, bm // dil, stride=dil), :] = load_residue(r)
        return tmp_ref[slot]

    l1 = l1_ref[...]
    l4 = to_token_order(t4_ref, N_HEADS_A, lambda r: l4_ref[r], 4)
    l16 = to_token_order(t16_ref, N_HEADS_A, lambda r: l16_ref[r], 16)
    mx = jnp.maximum(jnp.maximum(l1, l4), l16)
    e1, e4, e16 = jnp.exp2(l1 - mx), jnp.exp2(l4 - mx), jnp.exp2(l16 - mx)
    den = e1 + e4 + e16
    w1, w4, w16 = e1 / den, e4 / den, e16 / den
    ss = jnp.zeros((bm, HEAD_DIM), F32)
    for h in range(N_HEADS_A):
        o4 = to_token_order(t4_ref, h, lambda r: o4_ref[h, r].astype(F32), 4)
        o16 = to_token_order(t16_ref, h, lambda r: o16_ref[h, r].astype(F32), 16)
        oa = (w1[:, h:h + 1] * o1_ref[h].astype(F32) + w4[:, h:h + 1] * o4
              + w16[:, h:h + 1] * o16)
        oa_ref[:, h * HEAD_DIM:(h + 1) * HEAD_DIM] = oa
        ss = ss + oa * oa
    ra = lax.rsqrt(jnp.sum(ss, axis=-1, keepdims=True) / wa + EPS)
    mixed_ref[:, :wa] = (oa_ref[...] * ra * ona_ref[...]).astype(BF16)
    ob = ob_ref[...].astype(F32)
    rb = lax.rsqrt(jnp.mean(ob * ob, axis=-1, keepdims=True) + EPS)
    mixed_ref[:, wa:wa + wb] = (ob * rb * onb_ref[...]).astype(BF16)
    out_ref[...] = x_ref[...] + jnp.dot(mixed_ref[...], w_ref[...], preferred_element_type=F32)


def _outproj(o_branches, lse_branches, ob, on_a, on_b, w_all, layer, x2, batch, seq):
    t, d = x2.shape
    bm = _pick(seq, 256)
    nps = seq // bm
    wa = N_HEADS_A * HEAD_DIM
    wb = ob.shape[1]
    o1, o4, o16 = o_branches
    l1, l4, l16 = lse_branches
    o1 = o1.reshape(N_HEADS_A, t, HEAD_DIM)
    l1 = l1.reshape(t, LANES)

    def o_spec(dil):
        return pl.BlockSpec((N_HEADS_A, None, dil, bm // dil, HEAD_DIM),
                            lambda i: (0, i // nps, 0, i % nps, 0))

    def l_spec(dil):
        return pl.BlockSpec((None, dil, bm // dil, LANES), lambda i: (i // nps, 0, i % nps, 0))

    def rm(a, dil):
        return a.reshape(a.shape[:-2] + (dil, seq // dil, a.shape[-1]))

    return pl.pallas_call(
        _outproj_kernel,
        grid=(t // bm,),
        in_specs=[
            pl.BlockSpec((N_HEADS_A, bm, HEAD_DIM), lambda i: (0, i, 0)), o_spec(4), o_spec(16),
            pl.BlockSpec((bm, LANES), lambda i: (i, 0)), l_spec(4), l_spec(16),
            pl.BlockSpec((bm, wb), lambda i: (i, 0)),
            pl.BlockSpec((1, wa), lambda i: (0, 0)),
            pl.BlockSpec((1, wb), lambda i: (0, 0)),
            pl.BlockSpec((None, wa + wb, d), lambda i: (layer, 0, 0), pipeline_mode=pl.Buffered(1)),
            pl.BlockSpec((bm, d), lambda i: (i, 0)),
        ],
        out_specs=pl.BlockSpec((bm, d), lambda i: (i, 0)),
        out_shape=jax.ShapeDtypeStruct((t, d), F32),
        scratch_shapes=[pltpu.VMEM((bm, wa + wb), BF16), pltpu.VMEM((bm, wa), F32),
                        pltpu.VMEM((N_HEADS_A + 1, bm, HEAD_DIM), F32),
                        pltpu.VMEM((N_HEADS_A + 1, bm, HEAD_DIM), F32)],
        compiler_params=_params("parallel"),
        name="outproj",
    )(o1, rm(o4, 4), rm(o16, 16), l1, rm(l4, 4), rm(l16, 16), ob, on_a, on_b, w_all, x2)


def _swiglu_step(h, wg_ref, wu_ref, wd_ref, acc_ref):
    g = jnp.dot(h, wg_ref[...], preferred_element_type=F32)
    u = jnp.dot(h, wu_ref[...], preferred_element_type=F32)
    a = (g * jax.nn.sigmoid(g) * u).astype(BF16)
    acc_ref[...] += jnp.dot(a, wd_ref[...], preferred_element_type=F32)


CAST_COLS = 512


def _cast_chunks(n_steps, f, rows_gu, rows_d):
    ncb = f // CAST_COLS
    assert f % CAST_COLS == 0 and n_steps % ncb == 0
    nrg = n_steps // ncb
    assert rows_gu % nrg == 0 and rows_d % n_steps == 0
    rpc, rpd = rows_gu // nrg, rows_d // n_steps
    assert rpc % 16 == 0 and rpd % 16 == 0
    return ncb, rpc, rpd


def _dense_ffn_kernel(x_ref, g_ref, wg_ref, wu_ref, wd_ref, eg_ref, eu_ref, ed_ref,
                      out_ref, og_ref, ou_ref, od_ref,
                      hn_ref, acc_ref, ing_ref, inu_ref, ind_ref, outg_ref, outu_ref, outd_ref,
                      sem_in, sem_out, *, li, ncb, rpc, rpd):
    i = pl.program_id(0)
    j = pl.program_id(1)
    nj = pl.num_programs(1)
    step = i * nj + j
    last = pl.num_programs(0) * nj - 1
    slot = step % 2

    def in_copies(s, sl):
        rg = s // ncb
        cb = s - rg * ncb
        r0 = pl.multiple_of(rg * rpc, 16)
        c0 = pl.multiple_of(cb * CAST_COLS, CAST_COLS)
        d0 = pl.multiple_of(s * rpd, 16)
        return (pltpu.make_async_copy(eg_ref.at[li, pl.ds(r0, rpc), pl.ds(c0, CAST_COLS)],
                                      ing_ref.at[sl], sem_in.at[sl, 0]),
                pltpu.make_async_copy(eu_ref.at[li, pl.ds(r0, rpc), pl.ds(c0, CAST_COLS)],
                                      inu_ref.at[sl], sem_in.at[sl, 1]),
                pltpu.make_async_copy(ed_ref.at[li, pl.ds(d0, rpd), :], ind_ref.at[sl], sem_in.at[sl, 2]))

    def out_copies(s, sl):
        rg = s // ncb
        cb = s - rg * ncb
        r0 = pl.multiple_of(rg * rpc, 16)
        c0 = pl.multiple_of(cb * CAST_COLS, CAST_COLS)
        d0 = pl.multiple_of(s * rpd, 16)
        return (pltpu.make_async_copy(outg_ref.at[sl], og_ref.at[pl.ds(r0, rpc), pl.ds(c0, CAST_COLS)],
                                      sem_out.at[sl, 0]),
                pltpu.make_async_copy(outu_ref.at[sl], ou_ref.at[pl.ds(r0, rpc), pl.ds(c0, CAST_COLS)],
                                      sem_out.at[sl, 1]),
                pltpu.make_async_copy(outd_ref.at[sl], od_ref.at[pl.ds(d0, rpd), :], sem_out.at[sl, 2]))

    @pl.when(step == 0)
    def _():
        for c in in_copies(step, slot):
            c.start()

    @pl.when(step < last)
    def _():
        for c in in_copies(step + 1, 1 - slot):
            c.start()

    @pl.when(step >= 2)
    def _():
        for c in out_copies(step - 2, slot):
            c.wait()

    @pl.when(j == 0)
    def _():
        _rms_to_bf16(x_ref, g_ref, hn_ref)
        acc_ref[...] = jnp.zeros_like(acc_ref)

    for c in in_copies(step, slot):
        c.wait()
    outg_ref[slot] = ing_ref[slot].astype(BF16)
    outu_ref[slot] = inu_ref[slot].astype(BF16)
    outd_ref[slot] = ind_ref[slot].astype(BF16)
    _swiglu_step(hn_ref[...], wg_ref, wu_ref, wd_ref, acc_ref)
    for c in out_copies(step, slot):
        c.start()

    @pl.when(j == nj - 1)
    def _():
        out_ref[...] = x_ref[...] + acc_ref[...]

    @pl.when(step == last)
    def _():
        for c in out_copies(step, slot):
            c.wait()

        @pl.when(step >= 1)
        def _():
            for c in out_copies(step - 1, 1 - slot):
                c.wait()


def _dense_ffn(x2, g, wg, wu, wd, li, eg, eu, ed):
    t, d = x2.shape
    f = wg.shape[2]
    n_layers, n_exp, de, fe = eg.shape
    bm = _pick(t, 1024)
    bf = _pick(f, 256)
    grid = (t // bm, f // bf)
    rows_gu, rows_d = n_exp * de, n_exp * fe
    ncb, rpc, rpd = _cast_chunks(grid[0] * grid[1], fe, rows_gu, rows_d)
    single = pl.Buffered(1)
    any_spec = pl.BlockSpec(memory_space=pl.ANY)
    out, og, ou, od = pl.pallas_call(
        functools.partial(_dense_ffn_kernel, li=li, ncb=ncb, rpc=rpc, rpd=rpd),
        grid=grid,
        in_specs=[
            pl.BlockSpec((bm, d), lambda i, j: (i, 0), pipeline_mode=single),
            pl.BlockSpec((1, d), lambda i, j: (0, 0)),
            pl.BlockSpec((None, d, bf), lambda i, j: (li, 0, j)),
            pl.BlockSpec((None, d, bf), lambda i, j: (li, 0, j)),
            pl.BlockSpec((None, bf, d), lambda i, j: (li, j, 0)),
            any_spec, any_spec, any_spec,
        ],
        out_specs=[pl.BlockSpec((bm, d), lambda i, j: (i, 0), pipeline_mode=single),
                   any_spec, any_spec, any_spec],
        out_shape=[jax.ShapeDtypeStruct((t, d), F32),
                   jax.ShapeDtypeStruct((rows_gu, fe), BF16), jax.ShapeDtypeStruct((rows_gu, fe), BF16),
                   jax.ShapeDtypeStruct((rows_d, de), BF16)],
        scratch_shapes=[pltpu.VMEM((bm, d), BF16), pltpu.VMEM((bm, d), F32),
                        pltpu.VMEM((2, rpc, CAST_COLS), F32), pltpu.VMEM((2, rpc, CAST_COLS), F32),
                        pltpu.VMEM((2, rpd, de), F32),
                        pltpu.VMEM((2, rpc, CAST_COLS), BF16), pltpu.VMEM((2, rpc, CAST_COLS), BF16),
                        pltpu.VMEM((2, rpd, de), BF16),
                        pltpu.SemaphoreType.DMA((2, 3)), pltpu.SemaphoreType.DMA((2, 3))],
        compiler_params=_params("arbitrary", "arbitrary"),
        name="dense_ffn",
    )(x2, g, wg, wu, wd, eg.reshape(n_layers, rows_gu, fe), eu.reshape(n_layers, rows_gu, fe),
      ed.reshape(n_layers, rows_d, de))
    return (out, og.reshape(1, n_exp, de, fe), ou.reshape(1, n_exp, de, fe),
            od.reshape(1, n_exp, fe, de))


INFO_E1, INFO_E2, INFO_R1, INFO_R2, INFO_G1, INFO_G2 = range(6)


def _router_kernel(x_ref, g_ref, wr_ref, hn_ref, info_ref, cnt_ref, run_ref):
    i = pl.program_id(0)
    bm = x_ref.shape[0]

    @pl.when(i == 0)
    def _():
        run_ref[...] = jnp.zeros_like(run_ref)

    x = x_ref[...]
    ms = jnp.mean(x * x, axis=-1, keepdims=True)
    hn = x * lax.rsqrt(ms + EPS) * g_ref[...]
    hn_ref[...] = hn
    logits = jnp.dot(hn, wr_ref[...], preferred_element_type=F32, precision=lax.Precision.HIGHEST)
    lane = lax.broadcasted_iota(jnp.int32, (bm, LANES), 1)
    lg = jnp.where(lane < N_EXPERTS, logits, -jnp.inf)
    v1 = jnp.max(lg, axis=-1, keepdims=True)
    i1 = jnp.min(jnp.where(lg == v1, lane, LANES), axis=-1, keepdims=True)
    lg2 = jnp.where(lane == i1, -jnp.inf, lg)
    v2 = jnp.max(lg2, axis=-1, keepdims=True)
    i2 = jnp.min(jnp.where(lg2 == v2, lane, LANES), axis=-1, keepdims=True)
    e = jnp.exp(v2 - v1)
    g1 = 1.0 / (1.0 + e)
    g2 = e / (1.0 + e)
    hot1 = lane == i1
    hot2 = lane == i2
    onehot = (hot1 | hot2).astype(F32)
    tri = (lax.broadcasted_iota(jnp.int32, (bm, bm), 1)
           < lax.broadcasted_iota(jnp.int32, (bm, bm), 0)).astype(BF16)
    rank = jnp.dot(tri, onehot.astype(BF16), preferred_element_type=F32) + run_ref[0:1, :]
    r1 = jnp.sum(jnp.where(hot1, rank, 0.0), axis=-1, keepdims=True)
    r2 = jnp.sum(jnp.where(hot2, rank, 0.0), axis=-1, keepdims=True)
    run_ref[...] = run_ref[...] + jnp.sum(onehot, axis=0, keepdims=True)
    info = jnp.zeros((bm, LANES), F32)
    for col, val in ((INFO_E1, i1.astype(F32)), (INFO_E2, i2.astype(F32)), (INFO_R1, r1),
                     (INFO_R2, r2), (INFO_G1, g1), (INFO_G2, g2)):
        info = jnp.where(lane == col, val, info)
    info_ref[...] = info
    cnt_ref[...] = run_ref[...]


def _router(x2, g, wr_pad):
    t, d = x2.shape
    bm = _pick(t, 512)
    return pl.pallas_call(
        _router_kernel,
        grid=(t // bm,),
        in_specs=[
            pl.BlockSpec((bm, d), lambda i: (i, 0)),
            pl.BlockSpec((1, d), lambda i: (0, 0)),
            pl.BlockSpec((d, LANES), lambda i: (0, 0)),
        ],
        out_specs=[
            pl.BlockSpec((bm, d), lambda i: (i, 0)),
            pl.BlockSpec((bm, LANES), lambda i: (i, 0)),
            pl.BlockSpec((8, LANES), lambda i: (0, 0)),
        ],
        out_shape=[jax.ShapeDtypeStruct((t, d), F32), jax.ShapeDtypeStruct((t, LANES), F32),
                   jax.ShapeDtypeStruct((8, LANES), F32)],
        scratch_shapes=[pltpu.VMEM((8, LANES), F32)],
        compiler_params=_params("arbitrary"),
        name="router",
    )(x2, g, wr_pad)


def _scatter_kernel(pos1_ref, pos2_ref, hn_ref, hs_in_ref, hs_ref, sem):
    del hs_in_ref
    bt = hn_ref.shape[0]

    def copies(k):
        src = hn_ref.at[pl.ds(k, 1), :]
        return (pltpu.make_async_copy(src, hs_ref.at[pl.ds(pos1_ref[0, 0, k], 1), :], sem.at[0]),
                pltpu.make_async_copy(src, hs_ref.at[pl.ds(pos2_ref[0, 0, k], 1), :], sem.at[1]))

    def start(k, c):
        a, b = copies(k)
        a.start()
        b.start()
        return c

    def wait(k, c):
        a, b = copies(k)
        a.wait()
        b.wait()
        return c

    lax.fori_loop(0, bt, start, 0, unroll=DMA_UNROLL)
    lax.fori_loop(0, bt, wait, 0, unroll=DMA_UNROLL)


def _scatter_rows(hn, pos1, pos2, n_rows):
    t, d = hn.shape
    bt = _pick(t, 512)
    smem_spec = pl.BlockSpec((1, 1, bt), lambda i: (i, 0, 0), memory_space=pltpu.SMEM)
    any_spec = pl.BlockSpec(memory_space=pl.ANY)
    return pl.pallas_call(
        _scatter_kernel,
        grid=(t // bt,),
        in_specs=[smem_spec, smem_spec, pl.BlockSpec((bt, d), lambda i: (i, 0)), any_spec],
        out_specs=any_spec,
        out_shape=jax.ShapeDtypeStruct((n_rows, d), hn.dtype),
        scratch_shapes=[pltpu.SemaphoreType.DMA((2,))],
        input_output_aliases={3: 0},
        compiler_params=_params("arbitrary"),
        name="scatter_rows",
    )(pos1.reshape(t // bt, 1, bt), pos2.reshape(t // bt, 1, bt), hn, jnp.zeros((n_rows, d), hn.dtype))


def _grouped_ffn_kernel(be_ref, na_ref, hs_ref, wg_ref, wu_ref, wd_ref, out_ref, hb_ref, acc_ref):
    del be_ref
    i = pl.program_id(0)
    j = pl.program_id(1)
    active = i < na_ref[0]

    @pl.when(active & (j == 0))
    def _():
        hb_ref[...] = hs_ref[...].astype(BF16)
        acc_ref[...] = jnp.zeros_like(acc_ref)

    @pl.when(active)
    def _():
        _swiglu_step(hb_ref[...], wg_ref, wu_ref, wd_ref, acc_ref)

    @pl.when(j == pl.num_programs(1) - 1)
    def _():
        @pl.when(active)
        def _():
            out_ref[...] = acc_ref[...]

        @pl.when(jnp.logical_not(active))
        def _():
            out_ref[...] = jnp.zeros_like(out_ref)


def _grouped_ffn(hs, block_expert, n_active, wg, wu, wd, li, bm):
    p, d = hs.shape
    f = wg.shape[3]
    bf = _pick(f, 512)
    nf = f // bf

    def jj(i, j, na):
        return jnp.where(i < na[0], j, nf - 1)

    grid_spec = pltpu.PrefetchScalarGridSpec(
        num_scalar_prefetch=2,
        grid=(p // bm, nf),
        in_specs=[
            pl.BlockSpec((bm, d), lambda i, j, be, na: (jnp.maximum(jnp.minimum(i, na[0] - 1), 0), 0)),
            pl.BlockSpec((None, None, d, bf), lambda i, j, be, na: (li, be[i], 0, jj(i, j, na))),
            pl.BlockSpec((None, None, d, bf), lambda i, j, be, na: (li, be[i], 0, jj(i, j, na))),
            pl.BlockSpec((None, None, bf, d), lambda i, j, be, na: (li, be[i], jj(i, j, na), 0)),
        ],
        out_specs=pl.BlockSpec((bm, d), lambda i, j, be, na: (i, 0)),
        scratch_shapes=[pltpu.VMEM((bm, d), BF16), pltpu.VMEM((bm, d), F32)],
    )
    return pl.pallas_call(
        _grouped_ffn_kernel,
        grid_spec=grid_spec,
        out_shape=jax.ShapeDtypeStruct((p, d), F32),
        compiler_params=_params("arbitrary", "arbitrary"),
        name="grouped_ffn",
    )(block_expert, n_active, hs, wg, wu, wd)


def _combine_kernel(pos1_ref, pos2_ref, x_ref, info_ref, y_ref, out_ref, buf1_ref, buf2_ref, sem):
    bt = x_ref.shape[0]

    def copies(k):
        return (pltpu.make_async_copy(y_ref.at[pl.ds(pos1_ref[0, 0, k], 1), :],
                                      buf1_ref.at[pl.ds(k, 1), :], sem.at[0]),
                pltpu.make_async_copy(y_ref.at[pl.ds(pos2_ref[0, 0, k], 1), :],
                                      buf2_ref.at[pl.ds(k, 1), :], sem.at[1]))

    def start(k, c):
        a, b = copies(k)
        a.start()
        b.start()
        return c

    def wait(k, c):
        a, b = copies(k)
        a.wait()
        b.wait()
        return c

    lax.fori_loop(0, bt, start, 0, unroll=DMA_UNROLL)
    lax.fori_loop(0, bt, wait, 0, unroll=DMA_UNROLL)
    info = info_ref[...]
    g1 = info[:, INFO_G1:INFO_G1 + 1]
    g2 = info[:, INFO_G2:INFO_G2 + 1]
    out_ref[...] = x_ref[...] + g1 * buf1_ref[...] + g2 * buf2_ref[...]


def _combine(x2, info, y, pos1, pos2):
    t, d = x2.shape
    bt = _pick(t, 256)
    smem_spec = pl.BlockSpec((1, 1, bt), lambda i: (i, 0, 0), memory_space=pltpu.SMEM)
    return pl.pallas_call(
        _combine_kernel,
        grid=(t // bt,),
        in_specs=[
            smem_spec, smem_spec,
            pl.BlockSpec((bt, d), lambda i: (i, 0)),
            pl.BlockSpec((bt, LANES), lambda i: (i, 0)),
            pl.BlockSpec(memory_space=pl.ANY),
        ],
        out_specs=pl.BlockSpec((bt, d), lambda i: (i, 0)),
        out_shape=jax.ShapeDtypeStruct((t, d), F32),
        scratch_shapes=[pltpu.VMEM((bt, d), F32), pltpu.VMEM((bt, d), F32),
                        pltpu.SemaphoreType.DMA((2,))],
        compiler_params=_params("arbitrary"),
        name="combine",
    )(pos1.reshape(t // bt, 1, bt), pos2.reshape(t // bt, 1, bt), x2, info, y)


MOE_BLOCK_ROWS = 512


def _moe_ffn(x2, g, wr, wg, wu, wd, li):
    t, d = x2.shape
    bm = MOE_BLOCK_ROWS
    wr_pad = jnp.zeros((d, LANES), F32).at[:, :N_EXPERTS].set(wr)
    hn, info, cnt = _router(x2, g, wr_pad)
    e1 = info[:, INFO_E1].astype(jnp.int32)
    e2 = info[:, INFO_E2].astype(jnp.int32)
    r1 = info[:, INFO_R1].astype(jnp.int32)
    r2 = info[:, INFO_R2].astype(jnp.int32)
    counts = cnt[0, :N_EXPERTS].astype(jnp.int32)
    blocks_per_expert = (counts + bm - 1) // bm
    block_end = jnp.cumsum(blocks_per_expert)
    offsets = (block_end - blocks_per_expert) * bm
    pos1 = offsets[e1] + r1
    pos2 = offsets[e2] + r2
    n_blocks = (2 * t) // bm + N_EXPERTS
    n_active = block_end[-1:]
    ids = jnp.minimum(jnp.arange(n_blocks, dtype=jnp.int32), n_active[0] - 1)
    block_expert = jnp.sum((ids[:, None] >= block_end[None, :]).astype(jnp.int32), axis=1)
    block_expert = jnp.minimum(block_expert, N_EXPERTS - 1)
    hs = _scatter_rows(hn, pos1, pos2, n_blocks * bm)
    y = _grouped_ffn(hs, block_expert, n_active.astype(jnp.int32), wg, wu, wd, li, bm)
    return _combine(x2, info, y, pos1, pos2)


def _attention(proj_a, proj_b, batch, seq, shifted):
    views = [a.reshape(N_COLS_A, batch, seq, HEAD_DIM) for a in proj_a]
    branches = [_dilated_branch(v, batch, seq, dil, shifted) for v, dil in zip(views, DILATIONS)]
    ob = _gqa(proj_b, batch, seq, shifted)
    return [b[0] for b in branches], [b[1] for b in branches], ob


def kernel(x, g_mix, w_in, qn_a, kn_a, qn_b, kn_b, on_a, on_b, w_out, g_ffn, w_gate, w_up, w_down,
           w_router, we_gate, we_up, we_down):
    batch, seq, d = x.shape
    depth = w_in.shape[0]
    t = batch * seq
    assert all(w // (2 * dil) == BAND_HALF for w, dil in DILATED_PATTERNS) and DILATIONS == (1, 4, 16)
    assert w_in.shape[2] == (N_COLS_A + N_COLS_B) * HEAD_DIM
    scale = HEAD_DIM ** -0.5 * LOG2_E
    tables = _rope_tables(seq)
    w_in_b, w_out_b = w_in.astype(BF16), w_out.astype(BF16)
    w_gate_b, w_up_b, w_down_b = w_gate.astype(BF16), w_up.astype(BF16), w_down.astype(BF16)
    assert depth % 2 == 0
    x2 = x.reshape(t, d)
    for l in range(depth):
        gains = jnp.concatenate([jnp.tile(qn_a[l] * scale, N_HEADS_A), jnp.tile(kn_a[l], N_HEADS_A),
                                 jnp.ones((N_HEADS_A * HEAD_DIM,), F32),
                                 jnp.tile(qn_b[l] * scale, N_HEADS_B), jnp.tile(kn_b[l], N_KV_B),
                                 jnp.ones((N_KV_B * HEAD_DIM,), F32)])[None, :]
        bound = 1.02 * HEAD_DIM ** 0.5 * jnp.maximum(
            jnp.max(jnp.abs(qn_a[l])) * jnp.max(jnp.abs(kn_a[l])),
            jnp.max(jnp.abs(qn_b[l])) * jnp.max(jnp.abs(kn_b[l])))
        a1, a4, a16, proj_b = _inproj(x2, g_mix[l][None, :], w_in_b, l, gains,
                                      tables[0] + tables[1], batch, seq)
        attend = functools.partial(_attention, (a1, a4, a16), proj_b, batch, seq)
        o_br, lse_br, ob = lax.cond(bound <= MAX_UNSHIFTED_SCORE, lambda: attend(False),
                                    lambda: attend(True))
        x2 = _outproj(o_br, lse_br, ob, on_a[l][None, :], on_b[l][None, :], w_out_b, l, x2, batch, seq)
        i = l // 2
        if l % 2 == 0:
            x2, *experts_b = _dense_ffn(x2, g_ffn[l][None, :], w_gate_b, w_up_b, w_down_b, i,
                                        we_gate, we_up, we_down)
        else:
            x2 = _moe_ffn(x2, g_ffn[l][None, :], w_router[i], *experts_b, 0)
    return x2.reshape(batch, seq, d)
```

```python
import functools

import jax
import jax.numpy as jnp
from jax import lax
from jax.experimental import pallas as pl
from jax.experimental.pallas import tpu as pltpu

HEAD_DIM = 128
N_HEADS_A = 8
N_HEADS_B = 8
N_KV_B = 2
GQA_REP = N_HEADS_B // N_KV_B
N_COLS_A = 3 * N_HEADS_A
N_COLS_B = N_HEADS_B + 2 * N_KV_B
DILATED_PATTERNS = ((128, 1), (512, 4), (2048, 16))
DILATIONS = tuple(d for _, d in DILATED_PATTERNS)
BAND_HALF = 64
GRID_W = 64
ROPE_THETA = 10000.0
N_EXPERTS = 8
EPS = 1e-6
NEG = -1e30
MAX_UNSHIFTED_SCORE = 40.0
LOG2_E = 1.4426950408889634

HEADS_PER_STEP = 4
LANES = 128
DMA_UNROLL = 8
V7X_VMEM_LIMIT_BYTES = 56 * 1024 * 1024

BF16 = jnp.bfloat16
F32 = jnp.float32


def _params(*semantics):
    return pltpu.CompilerParams(dimension_semantics=semantics, vmem_limit_bytes=V7X_VMEM_LIMIT_BYTES)


def _pick(n, pref):
    b = min(n, pref)
    assert n % b == 0, (n, pref)
    return b


def _rope_angles(pos, dim):
    inv = ROPE_THETA ** (-jnp.arange(0, dim, 2, dtype=F32) / dim)
    ang = pos.astype(F32)[:, None] * inv[None, :]
    return jnp.cos(ang), jnp.sin(ang)


def _rope_tables(seq):
    pos = jnp.arange(seq)
    c1, s1 = _rope_angles(pos, HEAD_DIM)
    cos_a = jnp.concatenate([c1, c1], axis=-1)
    sin_a = jnp.concatenate([-s1, s1], axis=-1)
    cr, sr = _rope_angles(pos // GRID_W, HEAD_DIM // 2)
    cc, sc = _rope_angles(pos % GRID_W, HEAD_DIM // 2)
    z = jnp.zeros_like(sr)
    cos_b = jnp.concatenate([cr, cr, cc, cc], axis=-1)
    sin_b_up = jnp.concatenate([-sr, z, -sc, z], axis=-1)
    sin_b_dn = jnp.concatenate([z, sr, z, sc], axis=-1)
    return (cos_a, sin_a), (cos_b, sin_b_up, sin_b_dn)


def _rms_to_bf16(x_ref, g_ref, hn_ref):
    x = x_ref[...]
    ms = jnp.mean(x * x, axis=-1, keepdims=True)
    hn_ref[...] = (x * lax.rsqrt(ms + EPS) * g_ref[...]).astype(BF16)


def _head(y, hh):
    return y[:, hh * HEAD_DIM:(hh + 1) * HEAD_DIM]


def _inproj_kernel(x_ref, g_ref, w_ref, gain_ref, cosa_ref, sina_ref, cosb_ref, sinbu_ref, sinbd_ref,
                   a1_ref, a4_ref, a16_ref, b_ref, hn_ref, tmp_ref):
    bm = x_ref.shape[0]
    _rms_to_bf16(x_ref, g_ref, hn_ref)
    width = HEADS_PER_STEP * HEAD_DIM
    for c in range((N_COLS_A + N_COLS_B) // HEADS_PER_STEP):
        y = jnp.dot(hn_ref[...], w_ref[:, c * width:(c + 1) * width], preferred_element_type=F32)
        for hh in range(HEADS_PER_STEP):
            col = c * HEADS_PER_STEP + hh
            gain = gain_ref[:, col * HEAD_DIM:(col + 1) * HEAD_DIM]
            yh = _head(y, hh)
            is_a = col < N_COLS_A
            is_v = (2 * N_HEADS_A <= col < N_COLS_A) or col >= N_COLS_A + N_HEADS_B + N_KV_B
            if is_v:
                res = yh
            else:
                ms = jnp.mean(yh * yh, axis=-1, keepdims=True)
                yn = yh * lax.rsqrt(ms + EPS) * gain
                if is_a:
                    res = yn * cosa_ref[...] + pltpu.roll(yn, 64, 1) * sina_ref[...]
                else:
                    res = (yn * cosb_ref[...] + pltpu.roll(yn, 96, 1) * sinbu_ref[...]
                           + pltpu.roll(yn, 32, 1) * sinbd_ref[...])
            if is_a:
                a1_ref[col] = res.astype(BF16)
                tmp_ref[col] = res
                for out_ref, dil in ((a4_ref, 4), (a16_ref, 16)):
                    for r in range(dil):
                        out_ref[col, r] = tmp_ref[col, pl.ds(r, bm // dil, stride=dil), :].astype(BF16)
            else:
                b_ref[col - N_COLS_A] = res.astype(BF16)


def _inproj(x2, g, w_all, layer, gain_cols, tables, batch, seq):
    t, d = x2.shape
    bm = _pick(seq, 256)
    nps = seq // bm
    n = w_all.shape[2]
    tab_spec = pl.BlockSpec((bm, HEAD_DIM), lambda i: (i % nps, 0))

    def rm_spec(dil):
        return pl.BlockSpec((N_COLS_A, None, dil, bm // dil, HEAD_DIM),
                            lambda i: (0, i // nps, 0, i % nps, 0))

    def rm_shape(dil):
        return jax.ShapeDtypeStruct((N_COLS_A, batch, dil, seq // dil, HEAD_DIM), BF16)

    return pl.pallas_call(
        _inproj_kernel,
        grid=(t // bm,),
        in_specs=[
            pl.BlockSpec((bm, d), lambda i: (i, 0)),
            pl.BlockSpec((1, d), lambda i: (0, 0)),
            pl.BlockSpec((None, d, n), lambda i: (layer, 0, 0), pipeline_mode=pl.Buffered(1)),
            pl.BlockSpec((1, n), lambda i: (0, 0)),
            tab_spec, tab_spec, tab_spec, tab_spec, tab_spec,
        ],
        out_specs=[pl.BlockSpec((N_COLS_A, bm, HEAD_DIM), lambda i: (0, i, 0)), rm_spec(4), rm_spec(16),
                   pl.BlockSpec((N_COLS_B, bm, HEAD_DIM), lambda i: (0, i, 0))],
        out_shape=[jax.ShapeDtypeStruct((N_COLS_A, t, HEAD_DIM), BF16), rm_shape(4), rm_shape(16),
                   jax.ShapeDtypeStruct((N_COLS_B, t, HEAD_DIM), BF16)],
        scratch_shapes=[pltpu.VMEM((bm, d), BF16), pltpu.VMEM((N_COLS_A, bm, HEAD_DIM), F32)],
        compiler_params=_params("parallel"),
        name="inproj",
    )(x2, g, w_all, gain_cols, *tables)


QB_DIL = 128
KW_DIL = QB_DIL + 2 * BAND_HALF
BAND_SHIFT = BAND_HALF.bit_length() - 1
assert 1 << BAND_SHIFT == BAND_HALF
N_WINDOW_PLACEMENTS = QB_DIL // BAND_HALF + 1
DIL_UNROLL = 8


def _dilated_kernel(q_ref, k_ref, v_ref, o_ref, lse_ref, bias_ref, *, seg, shifted):
    h = pl.program_id(1)
    rows = q_ref.shape[0]
    lane = lax.broadcasted_iota(jnp.int32, (QB_DIL, LANES), 1)

    @pl.when(h == 0)
    def _():
        lse_ref[...] = jnp.zeros_like(lse_ref)
        rel0 = (lax.broadcasted_iota(jnp.int32, (QB_DIL, KW_DIL), 1)
                - lax.broadcasted_iota(jnp.int32, (QB_DIL, KW_DIL), 0))
        for c in range(bias_ref.shape[0]):
            bias_ref[c] = jnp.where(jnp.abs(rel0 - BAND_HALF * c) <= BAND_HALF, 0.0, NEG)

    def body(blk, carry):
        l0 = pl.multiple_of(blk * QB_DIL, QB_DIL)
        seg0 = l0 & (-seg)
        w0 = pl.multiple_of(jnp.clip(l0 - BAND_HALF, seg0, seg0 + seg - KW_DIL), BAND_HALF)
        q = q_ref[pl.ds(l0, QB_DIL), :]
        k = k_ref[pl.ds(w0, KW_DIL), :]
        v = v_ref[pl.ds(w0, KW_DIL), :]
        s = lax.dot_general(q, k, (((1,), (1,)), ((), ())), preferred_element_type=F32)
        s = s + bias_ref[lax.shift_right_logical(l0 - w0, BAND_SHIFT)]
        if shifted:
            m = jnp.max(s, axis=-1, keepdims=True)
            p = jnp.exp2(s - m)
        else:
            p = jnp.exp2(s)
        den = jnp.sum(p, axis=-1, keepdims=True)
        o = jnp.dot(p.astype(BF16), v, preferred_element_type=F32) / den
        lse = jnp.log2(den) + m if shifted else jnp.log2(den)
        o_ref[pl.ds(l0, QB_DIL), :] = o.astype(BF16)
        lse_ref[pl.ds(l0, QB_DIL), :] = jnp.where(lane == h, lse, lse_ref[pl.ds(l0, QB_DIL), :])
        return carry

    lax.fori_loop(0, rows // QB_DIL, body, 0, unroll=DIL_UNROLL)


def _dilated_branch(qkv_rm, batch, seq, dil, shifted):
    seg = seq // dil
    assert seg >= KW_DIL and seg % QB_DIL == 0 and seg & (seg - 1) == 0

    def spec(off):
        return pl.BlockSpec((None, None, seq, HEAD_DIM), lambda b, h: (off + h, b, 0, 0))

    return pl.pallas_call(
        functools.partial(_dilated_kernel, seg=seg, shifted=shifted),
        grid=(batch, N_HEADS_A),
        in_specs=[spec(0), spec(N_HEADS_A), spec(2 * N_HEADS_A)],
        out_specs=[pl.BlockSpec((None, None, seq, HEAD_DIM), lambda b, h: (h, b, 0, 0)),
                   pl.BlockSpec((None, seq, LANES), lambda b, h: (b, 0, 0))],
        out_shape=[jax.ShapeDtypeStruct((N_HEADS_A, batch, seq, HEAD_DIM), BF16),
                   jax.ShapeDtypeStruct((batch, seq, LANES), F32)],
        scratch_shapes=[pltpu.VMEM((N_WINDOW_PLACEMENTS, QB_DIL, KW_DIL), F32)],
        compiler_params=_params("parallel", "arbitrary"),
        name=f"dilated_d{dil}" + ("_shifted" if shifted else ""),
    )(qkv_rm, qkv_rm, qkv_rm)


def _gqa_kernel(q_ref, k_ref, v_ref, o_ref, *, bk, shifted):
    rep, bq, dh = q_ref.shape
    seq = k_ref.shape[0]
    rows = rep * bq
    q = q_ref[...].reshape(rows, dh)

    def scores(c):
        c0 = pl.multiple_of(c * bk, bk)
        k = k_ref[pl.ds(c0, bk), :]
        v = v_ref[pl.ds(c0, bk), :]
        return lax.dot_general(q, k, (((1,), (1,)), ((), ())), preferred_element_type=F32), v

    if shifted:
        def body(c, carry):
            m, l, acc = carry
            s, v = scores(c)
            m_new = jnp.maximum(m, jnp.max(s, axis=-1, keepdims=True))
            alpha = jnp.exp2(m - m_new)
            p = jnp.exp2(s - m_new)
            l = alpha * l + jnp.sum(p, axis=-1, keepdims=True)
            acc = alpha * acc + jnp.dot(p.astype(BF16), v, preferred_element_type=F32)
            return m_new, l, acc

        init = (jnp.full((rows, 1), NEG, F32), jnp.zeros((rows, 1), F32), jnp.zeros((rows, dh), F32))
        _, l, acc = lax.fori_loop(0, seq // bk, body, init)
    else:
        def body(c, carry):
            lsum, acc = carry
            s, v = scores(c)
            p = jnp.exp2(s)
            for tile in range(bk // LANES):
                lsum = lsum + p[:, tile * LANES:(tile + 1) * LANES]
            acc = acc + jnp.dot(p.astype(BF16), v, preferred_element_type=F32)
            return lsum, acc

        init = (jnp.zeros((rows, LANES), F32), jnp.zeros((rows, dh), F32))
        lsum, acc = lax.fori_loop(0, seq // bk, body, init, unroll=4)
        l = jnp.sum(lsum, axis=-1, keepdims=True)
    o = acc / l
    for h in range(rep):
        o_ref[:, h * dh:(h + 1) * dh] = o[h * bq:(h + 1) * bq].astype(o_ref.dtype)


def _gqa(proj_b, batch, seq, shifted):
    t = batch * seq
    bq = _pick(seq, 256)
    bk = _pick(seq, 512)
    nq = seq // bq
    view = proj_b.reshape(N_COLS_B, batch, seq, HEAD_DIM)
    return pl.pallas_call(
        functools.partial(_gqa_kernel, bk=bk, shifted=shifted),
        grid=(batch, N_KV_B, nq),
        in_specs=[
            pl.BlockSpec((GQA_REP, None, bq, HEAD_DIM), lambda b, g, i: (g, b, i, 0)),
            pl.BlockSpec((None, None, seq, HEAD_DIM), lambda b, g, i: (N_HEADS_B + g, b, 0, 0)),
            pl.BlockSpec((None, None, seq, HEAD_DIM), lambda b, g, i: (N_HEADS_B + N_KV_B + g, b, 0, 0)),
        ],
        out_specs=pl.BlockSpec((bq, GQA_REP * HEAD_DIM), lambda b, g, i: (b * nq + i, g)),
        out_shape=jax.ShapeDtypeStruct((t, N_HEADS_B * HEAD_DIM), BF16),
        compiler_params=_params("parallel", "parallel", "arbitrary"),
        name="gqa" + ("_shifted" if shifted else ""),
    )(view, view, view)


def _outproj_kernel(o1_ref, o4_ref, o16_ref, l1_ref, l4_ref, l16_ref, ob_ref, ona_ref, onb_ref,
                    w_ref, x_ref, out_ref, mixed_ref, oa_ref, t4_ref, t16_ref):
    wa = N_HEADS_A * HEAD_DIM
    wb = ob_ref.shape[1]
    bm = oa_ref.shape[0]

    def to_token_order(tmp_ref, slot, load_residue, dil):
        for r in range(dil):
            tmp_ref[slot, pl.ds(r, bm // dil, stride=dil), :] = load_residue(r)
        return tmp_ref[slot]

    l1 = l1_ref[...]
    l4 = to_token_order(t4_ref, N_HEADS_A, lambda r: l4_ref[r], 4)
    l16 = to_token_order(t16_ref, N_HEADS_A, lambda r: l16_ref[r], 16)
    mx = jnp.maximum(jnp.maximum(l1, l4), l16)
    e1, e4, e16 = jnp.exp2(l1 - mx), jnp.exp2(l4 - mx), jnp.exp2(l16 - mx)
    den = e1 + e4 + e16
    w1, w4, w16 = e1 / den, e4 / den, e16 / den
    ss = jnp.zeros((bm, HEAD_DIM), F32)
    for h in range(N_HEADS_A):
        o4 = to_token_order(t4_ref, h, lambda r: o4_ref[h, r].astype(F32), 4)
        o16 = to_token_order(t16_ref, h, lambda r: o16_ref[h, r].astype(F32), 16)
        oa = (w1[:, h:h + 1] * o1_ref[h].astype(F32) + w4[:, h:h + 1] * o4
              + w16[:, h:h + 1] * o16)
        oa_ref[:, h * HEAD_DIM:(h + 1) * HEAD_DIM] = oa
        ss = ss + oa * oa
    ra = lax.rsqrt(jnp.sum(ss, axis=-1, keepdims=True) / wa + EPS)
    mixed_ref[:, :wa] = (oa_ref[...] * ra * ona_ref[...]).astype(BF16)
    ob = ob_ref[...].astype(F32)
    rb = lax.rsqrt(jnp.mean(ob * ob, axis=-1, keepdims=True) + EPS)
    mixed_ref[:, wa:wa + wb] = (ob * rb * onb_ref[...]).astype(BF16)
    out_ref[...] = x_ref[...] + jnp.dot(mixed_ref[...], w_ref[...], preferred_element_type=F32)


def _outproj(o_branches, lse_branches, ob, on_a, on_b, w_all, layer, x2, batch, seq):
    t, d = x2.shape
    bm = _pick(seq, 256)
    nps = seq // bm
    wa = N_HEADS_A * HEAD_DIM
    wb = ob.shape[1]
    o1, o4, o16 = o_branches
    l1, l4, l16 = lse_branches
    o1 = o1.reshape(N_HEADS_A, t, HEAD_DIM)
    l1 = l1.reshape(t, LANES)

    def o_spec(dil):
        return pl.BlockSpec((N_HEADS_A, None, dil, bm // dil, HEAD_DIM),
                            lambda i: (0, i // nps, 0, i % nps, 0))

    def l_spec(dil):
        return pl.BlockSpec((None, dil, bm // dil, LANES), lambda i: (i // nps, 0, i % nps, 0))

    def rm(a, dil):
        return a.reshape(a.shape[:-2] + (dil, seq // dil, a.shape[-1]))

    return pl.pallas_call(
        _outproj_kernel,
        grid=(t // bm,),
        in_specs=[
            pl.BlockSpec((N_HEADS_A, bm, HEAD_DIM), lambda i: (0, i, 0)), o_spec(4), o_spec(16),
            pl.BlockSpec((bm, LANES), lambda i: (i, 0)), l_spec(4), l_spec(16),
            pl.BlockSpec((bm, wb), lambda i: (i, 0)),
            pl.BlockSpec((1, wa), lambda i: (0, 0)),
            pl.BlockSpec((1, wb), lambda i: (0, 0)),
            pl.BlockSpec((None, wa + wb, d), lambda i: (layer, 0, 0), pipeline_mode=pl.Buffered(1)),
            pl.BlockSpec((bm, d), lambda i: (i, 0)),
        ],
        out_specs=pl.BlockSpec((bm, d), lambda i: (i, 0)),
        out_shape=jax.ShapeDtypeStruct((t, d), F32),
        scratch_shapes=[pltpu.VMEM((bm, wa + wb), BF16), pltpu.VMEM((bm, wa), F32),
                        pltpu.VMEM((N_HEADS_A + 1, bm, HEAD_DIM), F32),
                        pltpu.VMEM((N_HEADS_A + 1, bm, HEAD_DIM), F32)],
        compiler_params=_params("parallel"),
        name="outproj",
    )(o1, rm(o4, 4), rm(o16, 16), l1, rm(l4, 4), rm(l16, 16), ob, on_a, on_b, w_all, x2)


def _swiglu_step(h, wg_ref, wu_ref, wd_ref, acc_ref):
    g = jnp.dot(h, wg_ref[...], preferred_element_type=F32)
    u = jnp.dot(h, wu_ref[...], preferred_element_type=F32)
    a = (g * jax.nn.sigmoid(g) * u).astype(BF16)
    acc_ref[...] += jnp.dot(a, wd_ref[...], preferred_element_type=F32)


CAST_COLS = 512


def _cast_chunks(n_steps, f, rows_gu, rows_d):
    ncb = f // CAST_COLS
    assert f % CAST_COLS == 0 and n_steps % ncb == 0
    nrg = n_steps // ncb
    assert rows_gu % nrg == 0 and rows_d % n_steps == 0
    rpc, rpd = rows_gu // nrg, rows_d // n_steps
    assert rpc % 16 == 0 and rpd % 16 == 0
    return ncb, rpc, rpd


def _dense_ffn_kernel(x_ref, g_ref, wg_ref, wu_ref, wd_ref, eg_ref, eu_ref, ed_ref,
                      out_ref, og_ref, ou_ref, od_ref,
                      hn_ref, ing_ref, inu_ref, ind_ref, outg_ref, outu_ref, outd_ref,
                      sem_in, sem_out, *, li, ncb, rpc, rpd):
    i = pl.program_id(0)
    j = pl.program_id(1)
    nj = pl.num_programs(1)
    step = i * nj + j
    last = pl.num_programs(0) * nj - 1
    slot = step % 2

    def in_copies(s, sl):
        rg = s // ncb
        cb = s - rg * ncb
        r0 = pl.multiple_of(rg * rpc, 16)
        c0 = pl.multiple_of(cb * CAST_COLS, CAST_COLS)
        d0 = pl.multiple_of(s * rpd, 16)
        return (pltpu.make_async_copy(eg_ref.at[li, pl.ds(r0, rpc), pl.ds(c0, CAST_COLS)],
                                      ing_ref.at[sl], sem_in.at[sl, 0]),
                pltpu.make_async_copy(eu_ref.at[li, pl.ds(r0, rpc), pl.ds(c0, CAST_COLS)],
                                      inu_ref.at[sl], sem_in.at[sl, 1]),
                pltpu.make_async_copy(ed_ref.at[li, pl.ds(d0, rpd), :], ind_ref.at[sl], sem_in.at[sl, 2]))

    def out_copies(s, sl):
        rg = s // ncb
        cb = s - rg * ncb
        r0 = pl.multiple_of(rg * rpc, 16)
        c0 = pl.multiple_of(cb * CAST_COLS, CAST_COLS)
        d0 = pl.multiple_of(s * rpd, 16)
        return (pltpu.make_async_copy(outg_ref.at[sl], og_ref.at[pl.ds(r0, rpc), pl.ds(c0, CAST_COLS)],
                                      sem_out.at[sl, 0]),
                pltpu.make_async_copy(outu_ref.at[sl], ou_ref.at[pl.ds(r0, rpc), pl.ds(c0, CAST_COLS)],
                                      sem_out.at[sl, 1]),
                pltpu.make_async_copy(outd_ref.at[sl], od_ref.at[pl.ds(d0, rpd), :], sem_out.at[sl, 2]))

    @pl.when(step == 0)
    def _():
        for c in in_copies(step, slot):
            c.start()

    @pl.when(step < last)
    def _():
        for c in in_copies(step + 1, 1 - slot):
            c.start()

    @pl.when(step >= 2)
    def _():
        for c in out_copies(step - 2, slot):
            c.wait()

    @pl.when(j == 0)
    def _():
        _rms_to_bf16(x_ref, g_ref, hn_ref)
        out_ref[...] = jnp.zeros_like(out_ref)

    for c in in_copies(step, slot):
        c.wait()
    outg_ref[slot] = ing_ref[slot].astype(BF16)
    outu_ref[slot] = inu_ref[slot].astype(BF16)
    outd_ref[slot] = ind_ref[slot].astype(BF16)
    _swiglu_step(hn_ref[...], wg_ref, wu_ref, wd_ref, out_ref)
    for c in out_copies(step, slot):
        c.start()

    @pl.when(j == nj - 1)
    def _():
        out_ref[...] += x_ref[...]

    @pl.when(step == last)
    def _():
        for c in out_copies(step, slot):
            c.wait()

        @pl.when(step >= 1)
        def _():
            for c in out_copies(step - 1, 1 - slot):
                c.wait()


def _dense_ffn(x2, g, wg, wu, wd, li, eg, eu, ed):
    t, d = x2.shape
    f = wg.shape[2]
    n_layers, n_exp, de, fe = eg.shape
    bm = _pick(t, 1024)
    bf = _pick(f, 256)
    grid = (t // bm, f // bf)
    rows_gu, rows_d = n_exp * de, n_exp * fe
    ncb, rpc, rpd = _cast_chunks(grid[0] * grid[1], fe, rows_gu, rows_d)
    any_spec = pl.BlockSpec(memory_space=pl.ANY)
    out, og, ou, od = pl.pallas_call(
        functools.partial(_dense_ffn_kernel, li=li, ncb=ncb, rpc=rpc, rpd=rpd),
        grid=grid,
        in_specs=[
            pl.BlockSpec((bm, d), lambda i, j: (i, 0)),
            pl.BlockSpec((1, d), lambda i, j: (0, 0)),
            pl.BlockSpec((None, d, bf), lambda i, j: (li, 0, j)),
            pl.BlockSpec((None, d, bf), lambda i, j: (li, 0, j)),
            pl.BlockSpec((None, bf, d), lambda i, j: (li, j, 0)),
            any_spec, any_spec, any_spec,
        ],
        out_specs=[pl.BlockSpec((bm, d), lambda i, j: (i, 0)), any_spec, any_spec, any_spec],
        out_shape=[jax.ShapeDtypeStruct((t, d), F32),
                   jax.ShapeDtypeStruct((rows_gu, fe), BF16), jax.ShapeDtypeStruct((rows_gu, fe), BF16),
                   jax.ShapeDtypeStruct((rows_d, de), BF16)],
        scratch_shapes=[pltpu.VMEM((bm, d), BF16),
                        pltpu.VMEM((2, rpc, CAST_COLS), F32), pltpu.VMEM((2, rpc, CAST_COLS), F32),
                        pltpu.VMEM((2, rpd, de), F32),
                        pltpu.VMEM((2, rpc, CAST_COLS), BF16), pltpu.VMEM((2, rpc, CAST_COLS), BF16),
                        pltpu.VMEM((2, rpd, de), BF16),
                        pltpu.SemaphoreType.DMA((2, 3)), pltpu.SemaphoreType.DMA((2, 3))],
        compiler_params=_params("arbitrary", "arbitrary"),
        name="dense_ffn",
    )(x2, g, wg, wu, wd, eg.reshape(n_layers, rows_gu, fe), eu.reshape(n_layers, rows_gu, fe),
      ed.reshape(n_layers, rows_d, de))
    return (out, og.reshape(1, n_exp, de, fe), ou.reshape(1, n_exp, de, fe),
            od.reshape(1, n_exp, fe, de))


INFO_E1, INFO_E2, INFO_R1, INFO_R2, INFO_G1, INFO_G2 = range(6)


def _router_kernel(x_ref, g_ref, wr_ref, hn_ref, info_ref, cnt_ref, run_ref):
    i = pl.program_id(0)
    bm = x_ref.shape[0]

    @pl.when(i == 0)
    def _():
        run_ref[...] = jnp.zeros_like(run_ref)

    x = x_ref[...]
    ms = jnp.mean(x * x, axis=-1, keepdims=True)
    hn = x * lax.rsqrt(ms + EPS) * g_ref[...]
    hn_ref[...] = hn
    logits = jnp.dot(hn, wr_ref[...], preferred_element_type=F32, precision=lax.Precision.HIGHEST)
    lane = lax.broadcasted_iota(jnp.int32, (bm, LANES), 1)
    lg = jnp.where(lane < N_EXPERTS, logits, -jnp.inf)
    v1 = jnp.max(lg, axis=-1, keepdims=True)
    i1 = jnp.min(jnp.where(lg == v1, lane, LANES), axis=-1, keepdims=True)
    lg2 = jnp.where(lane == i1, -jnp.inf, lg)
    v2 = jnp.max(lg2, axis=-1, keepdims=True)
    i2 = jnp.min(jnp.where(lg2 == v2, lane, LANES), axis=-1, keepdims=True)
    e = jnp.exp(v2 - v1)
    g1 = 1.0 / (1.0 + e)
    g2 = e / (1.0 + e)
    hot1 = lane == i1
    hot2 = lane == i2
    onehot = (hot1 | hot2).astype(F32)
    tri = (lax.broadcasted_iota(jnp.int32, (bm, bm), 1)
           < lax.broadcasted_iota(jnp.int32, (bm, bm), 0)).astype(BF16)
    rank = jnp.dot(tri, onehot.astype(BF16), preferred_element_type=F32) + run_ref[0:1, :]
    r1 = jnp.sum(jnp.where(hot1, rank, 0.0), axis=-1, keepdims=True)
    r2 = jnp.sum(jnp.where(hot2, rank, 0.0), axis=-1, keepdims=True)
    run_ref[...] = run_ref[...] + jnp.sum(onehot, axis=0, keepdims=True)
    info = jnp.zeros((bm, LANES), F32)
    for col, val in ((INFO_E1, i1.astype(F32)), (INFO_E2, i2.astype(F32)), (INFO_R1, r1),
                     (INFO_R2, r2), (INFO_G1, g1), (INFO_G2, g2)):
        info = jnp.where(lane == col, val, info)
    info_ref[...] = info
    cnt_ref[...] = run_ref[...]


def _router(x2, g, wr_pad):
    t, d = x2.shape
    bm = _pick(t, 512)
    return pl.pallas_call(
        _router_kernel,
        grid=(t // bm,),
        in_specs=[
            pl.BlockSpec((bm, d), lambda i: (i, 0)),
            pl.BlockSpec((1, d), lambda i: (0, 0)),
            pl.BlockSpec((d, LANES), lambda i: (0, 0)),
        ],
        out_specs=[
            pl.BlockSpec((bm, d), lambda i: (i, 0)),
            pl.BlockSpec((bm, LANES), lambda i: (i, 0)),
            pl.BlockSpec((8, LANES), lambda i: (0, 0)),
        ],
        out_shape=[jax.ShapeDtypeStruct((t, d), F32), jax.ShapeDtypeStruct((t, LANES), F32),
                   jax.ShapeDtypeStruct((8, LANES), F32)],
        scratch_shapes=[pltpu.VMEM((8, LANES), F32)],
        compiler_params=_params("arbitrary"),
        name="router",
    )(x2, g, wr_pad)


def _scatter_kernel(pos1_ref, pos2_ref, hn_ref, hs_in_ref, hs_ref, sem):
    del hs_in_ref
    bt = hn_ref.shape[0]

    def copies(k):
        src = hn_ref.at[pl.ds(k, 1), :]
        return (pltpu.make_async_copy(src, hs_ref.at[pl.ds(pos1_ref[0, 0, k], 1), :], sem.at[0]),
                pltpu.make_async_copy(src, hs_ref.at[pl.ds(pos2_ref[0, 0, k], 1), :], sem.at[1]))

    def start(k, c):
        a, b = copies(k)
        a.start()
        b.start()
        return c

    def wait(k, c):
        a, b = copies(k)
        a.wait()
        b.wait()
        return c

    lax.fori_loop(0, bt, start, 0, unroll=DMA_UNROLL)
    lax.fori_loop(0, bt, wait, 0, unroll=DMA_UNROLL)


def _scatter_rows(hn, pos1, pos2, n_rows):
    t, d = hn.shape
    bt = _pick(t, 512)
    smem_spec = pl.BlockSpec((1, 1, bt), lambda i: (i, 0, 0), memory_space=pltpu.SMEM)
    any_spec = pl.BlockSpec(memory_space=pl.ANY)
    return pl.pallas_call(
        _scatter_kernel,
        grid=(t // bt,),
        in_specs=[smem_spec, smem_spec, pl.BlockSpec((bt, d), lambda i: (i, 0)), any_spec],
        out_specs=any_spec,
        out_shape=jax.ShapeDtypeStruct((n_rows, d), hn.dtype),
        scratch_shapes=[pltpu.SemaphoreType.DMA((2,))],
        input_output_aliases={3: 0},
        compiler_params=_params("arbitrary"),
        name="scatter_rows",
    )(pos1.reshape(t // bt, 1, bt), pos2.reshape(t // bt, 1, bt), hn, jnp.zeros((n_rows, d), hn.dtype))


def _grouped_ffn_kernel(be_ref, na_ref, hs_ref, wg_ref, wu_ref, wd_ref, out_ref, hb_ref):
    del be_ref
    i = pl.program_id(0)
    j = pl.program_id(1)
    active = i < na_ref[0]

    @pl.when(j == 0)
    def _():
        out_ref[...] = jnp.zeros_like(out_ref)

    @pl.when(active & (j == 0))
    def _():
        hb_ref[...] = hs_ref[...].astype(BF16)

    @pl.when(active)
    def _():
        _swiglu_step(hb_ref[...], wg_ref, wu_ref, wd_ref, out_ref)


def _grouped_ffn(hs, block_expert, n_active, wg, wu, wd, li, bm):
    p, d = hs.shape
    f = wg.shape[3]
    bf = _pick(f, 512)
    nf = f // bf

    def jj(i, j, na):
        return jnp.where(i < na[0], j, nf - 1)

    grid_spec = pltpu.PrefetchScalarGridSpec(
        num_scalar_prefetch=2,
        grid=(p // bm, nf),
        in_specs=[
            pl.BlockSpec((bm, d), lambda i, j, be, na: (jnp.maximum(jnp.minimum(i, na[0] - 1), 0), 0)),
            pl.BlockSpec((None, None, d, bf), lambda i, j, be, na: (li, be[i], 0, jj(i, j, na))),
            pl.BlockSpec((None, None, d, bf), lambda i, j, be, na: (li, be[i], 0, jj(i, j, na))),
            pl.BlockSpec((None, None, bf, d), lambda i, j, be, na: (li, be[i], jj(i, j, na), 0)),
        ],
        out_specs=pl.BlockSpec((bm, d), lambda i, j, be, na: (i, 0)),
        scratch_shapes=[pltpu.VMEM((bm, d), BF16)],
    )
    return pl.pallas_call(
        _grouped_ffn_kernel,
        grid_spec=grid_spec,
        out_shape=jax.ShapeDtypeStruct((p, d), F32),
        compiler_params=_params("arbitrary", "arbitrary"),
        name="grouped_ffn",
    )(block_expert, n_active, hs, wg, wu, wd)


def _combine_kernel(pos1_ref, pos2_ref, x_ref, info_ref, y_ref, out_ref, buf1_ref, buf2_ref, sem):
    bt = x_ref.shape[0]

    def copies(k):
        return (pltpu.make_async_copy(y_ref.at[pl.ds(pos1_ref[0, 0, k], 1), :],
                                      buf1_ref.at[pl.ds(k, 1), :], sem.at[0]),
                pltpu.make_async_copy(y_ref.at[pl.ds(pos2_ref[0, 0, k], 1), :],
                                      buf2_ref.at[pl.ds(k, 1), :], sem.at[1]))

    def start(k, c):
        a, b = copies(k)
        a.start()
        b.start()
        return c

    def wait(k, c):
        a, b = copies(k)
        a.wait()
        b.wait()
        return c

    lax.fori_loop(0, bt, start, 0, unroll=DMA_UNROLL)
    lax.fori_loop(0, bt, wait, 0, unroll=DMA_UNROLL)
    info = info_ref[...]
    g1 = info[:, INFO_G1:INFO_G1 + 1]
    g2 = info[:, INFO_G2:INFO_G2 + 1]
    out_ref[...] = x_ref[...] + g1 * buf1_ref[...] + g2 * buf2_ref[...]


def _combine(x2, info, y, pos1, pos2):
    t, d = x2.shape
    bt = _pick(t, 512)
    smem_spec = pl.BlockSpec((1, 1, bt), lambda i: (i, 0, 0), memory_space=pltpu.SMEM)
    return pl.pallas_call(
        _combine_kernel,
        grid=(t // bt,),
        in_specs=[
            smem_spec, smem_spec,
            pl.BlockSpec((bt, d), lambda i: (i, 0)),
            pl.BlockSpec((bt, LANES), lambda i: (i, 0)),
            pl.BlockSpec(memory_space=pl.ANY),
        ],
        out_specs=pl.BlockSpec((bt, d), lambda i: (i, 0)),
        out_shape=jax.ShapeDtypeStruct((t, d), F32),
        scratch_shapes=[pltpu.VMEM((bt, d), F32), pltpu.VMEM((bt, d), F32),
                        pltpu.SemaphoreType.DMA((2,))],
        compiler_params=_params("arbitrary"),
        name="combine",
    )(pos1.reshape(t // bt, 1, bt), pos2.reshape(t // bt, 1, bt), x2, info, y)


MOE_BLOCK_ROWS = 512


def _moe_ffn(x2, g, wr, wg, wu, wd, li):
    t, d = x2.shape
    bm = MOE_BLOCK_ROWS
    wr_pad = jnp.zeros((d, LANES), F32).at[:, :N_EXPERTS].set(wr)
    hn, info, cnt = _router(x2, g, wr_pad)
    e1 = info[:, INFO_E1].astype(jnp.int32)
    e2 = info[:, INFO_E2].astype(jnp.int32)
    r1 = info[:, INFO_R1].astype(jnp.int32)
    r2 = info[:, INFO_R2].astype(jnp.int32)
    counts = cnt[0, :N_EXPERTS].astype(jnp.int32)
    blocks_per_expert = (counts + bm - 1) // bm
    block_end = jnp.cumsum(blocks_per_expert)
    offsets = (block_end - blocks_per_expert) * bm
    pos1 = offsets[e1] + r1
    pos2 = offsets[e2] + r2
    n_blocks = (2 * t) // bm + N_EXPERTS
    n_active = block_end[-1:]
    ids = jnp.minimum(jnp.arange(n_blocks, dtype=jnp.int32), n_active[0] - 1)
    block_expert = jnp.sum((ids[:, None] >= block_end[None, :]).astype(jnp.int32), axis=1)
    block_expert = jnp.minimum(block_expert, N_EXPERTS - 1)
    hs = _scatter_rows(hn, pos1, pos2, n_blocks * bm)
    y = _grouped_ffn(hs, block_expert, n_active.astype(jnp.int32), wg, wu, wd, li, bm)
    return _combine(x2, info, y, pos1, pos2)


def _attention(proj_a, proj_b, batch, seq, shifted):
    views = [a.reshape(N_COLS_A, batch, seq, HEAD_DIM) for a in proj_a]
    branches = [_dilated_branch(v, batch, seq, dil, shifted) for v, dil in zip(views, DILATIONS)]
    ob = _gqa(proj_b, batch, seq, shifted)
    return [b[0] for b in branches], [b[1] for b in branches], ob


def kernel(x, g_mix, w_in, qn_a, kn_a, qn_b, kn_b, on_a, on_b, w_out, g_ffn, w_gate, w_up, w_down,
           w_router, we_gate, we_up, we_down):
    batch, seq, d = x.shape
    depth = w_in.shape[0]
    t = batch * seq
    assert all(w // (2 * dil) == BAND_HALF for w, dil in DILATED_PATTERNS) and DILATIONS == (1, 4, 16)
    assert w_in.shape[2] == (N_COLS_A + N_COLS_B) * HEAD_DIM
    scale = HEAD_DIM ** -0.5 * LOG2_E
    tables = _rope_tables(seq)
    w_in_b, w_out_b = w_in.astype(BF16), w_out.astype(BF16)
    w_gate_b, w_up_b, w_down_b = w_gate.astype(BF16), w_up.astype(BF16), w_down.astype(BF16)
    assert depth % 2 == 0
    x2 = x.reshape(t, d)
    for l in range(depth):
        gains = jnp.concatenate([jnp.tile(qn_a[l] * scale, N_HEADS_A), jnp.tile(kn_a[l], N_HEADS_A),
                                 jnp.ones((N_HEADS_A * HEAD_DIM,), F32),
                                 jnp.tile(qn_b[l] * scale, N_HEADS_B), jnp.tile(kn_b[l], N_KV_B),
                                 jnp.ones((N_KV_B * HEAD_DIM,), F32)])[None, :]
        bound = 1.02 * HEAD_DIM ** 0.5 * jnp.maximum(
            jnp.max(jnp.abs(qn_a[l])) * jnp.max(jnp.abs(kn_a[l])),
            jnp.max(jnp.abs(qn_b[l])) * jnp.max(jnp.abs(kn_b[l])))
        a1, a4, a16, proj_b = _inproj(x2, g_mix[l][None, :], w_in_b, l, gains,
                                      tables[0] + tables[1], batch, seq)
        attend = functools.partial(_attention, (a1, a4, a16), proj_b, batch, seq)
        o_br, lse_br, ob = lax.cond(bound <= MAX_UNSHIFTED_SCORE, lambda: attend(False),
                                    lambda: attend(True))
        x2 = _outproj(o_br, lse_br, ob, on_a[l][None, :], on_b[l][None, :], w_out_b, l, x2, batch, seq)
        i = l // 2
        if l % 2 == 0:
            x2, *experts_b = _dense_ffn(x2, g_ffn[l][None, :], w_gate_b, w_up_b, w_down_b, i,
                                        we_gate, we_up, we_down)
        else:
            x2 = _moe_ffn(x2, g_ffn[l][None, :], w_router[i], *experts_b, 0)
    return x2.reshape(batch, seq, d)
```

```python
import functools

import jax
import jax.numpy as jnp
from jax import lax
from jax.experimental import pallas as pl
from jax.experimental.pallas import tpu as pltpu

HEAD_DIM = 128
N_HEADS_A = 8
N_HEADS_B = 8
N_KV_B = 2
GQA_REP = N_HEADS_B // N_KV_B
N_COLS_A = 3 * N_HEADS_A
N_COLS_B = N_HEADS_B + 2 * N_KV_B
DILATED_PATTERNS = ((128, 1), (512, 4), (2048, 16))
DILATIONS = tuple(d for _, d in DILATED_PATTERNS)
BAND_HALF = 64
GRID_W = 64
ROPE_THETA = 10000.0
N_EXPERTS = 8
EPS = 1e-6
NEG = -1e30
MAX_UNSHIFTED_SCORE = 40.0
LOG2_E = 1.4426950408889634

HEADS_PER_STEP = 4
LANES = 128
DMA_UNROLL = 8
V7X_VMEM_LIMIT_BYTES = 56 * 1024 * 1024

BF16 = jnp.bfloat16
F32 = jnp.float32


def _params(*semantics):
    return pltpu.CompilerParams(dimension_semantics=semantics, vmem_limit_bytes=V7X_VMEM_LIMIT_BYTES)


def _pick(n, pref):
    b = min(n, pref)
    assert n % b == 0, (n, pref)
    return b


def _rope_angles(pos, dim):
    inv = ROPE_THETA ** (-jnp.arange(0, dim, 2, dtype=F32) / dim)
    ang = pos.astype(F32)[:, None] * inv[None, :]
    return jnp.cos(ang), jnp.sin(ang)


def _rope_tables(seq):
    pos = jnp.arange(seq)
    c1, s1 = _rope_angles(pos, HEAD_DIM)
    cos_a = jnp.concatenate([c1, c1], axis=-1)
    sin_a = jnp.concatenate([-s1, s1], axis=-1)
    cr, sr = _rope_angles(pos // GRID_W, HEAD_DIM // 2)
    cc, sc = _rope_angles(pos % GRID_W, HEAD_DIM // 2)
    z = jnp.zeros_like(sr)
    cos_b = jnp.concatenate([cr, cr, cc, cc], axis=-1)
    sin_b_up = jnp.concatenate([-sr, z, -sc, z], axis=-1)
    sin_b_dn = jnp.concatenate([z, sr, z, sc], axis=-1)
    return (cos_a, sin_a), (cos_b, sin_b_up, sin_b_dn)


def _rms_to_bf16(x_ref, g_ref, hn_ref):
    x = x_ref[...]
    ms = jnp.mean(x * x, axis=-1, keepdims=True)
    hn_ref[...] = (x * lax.rsqrt(ms + EPS) * g_ref[...]).astype(BF16)


def _head(y, hh):
    return y[:, hh * HEAD_DIM:(hh + 1) * HEAD_DIM]


def _inproj_kernel(x_ref, g_ref, w_ref, gain_ref, cosa_ref, sina_ref, cosb_ref, sinbu_ref, sinbd_ref,
                   a1_ref, a4_ref, a16_ref, b_ref, hn_ref, tmp_ref):
    bm = x_ref.shape[0]
    _rms_to_bf16(x_ref, g_ref, hn_ref)
    width = HEADS_PER_STEP * HEAD_DIM
    for c in range((N_COLS_A + N_COLS_B) // HEADS_PER_STEP):
        y = jnp.dot(hn_ref[...], w_ref[:, c * width:(c + 1) * width], preferred_element_type=F32)
        for hh in range(HEADS_PER_STEP):
            col = c * HEADS_PER_STEP + hh
            gain = gain_ref[:, col * HEAD_DIM:(col + 1) * HEAD_DIM]
            yh = _head(y, hh)
            is_a = col < N_COLS_A
            is_v = (2 * N_HEADS_A <= col < N_COLS_A) or col >= N_COLS_A + N_HEADS_B + N_KV_B
            if is_v:
                res = yh
            else:
                ms = jnp.mean(yh * yh, axis=-1, keepdims=True)
                yn = yh * lax.rsqrt(ms + EPS) * gain
                if is_a:
                    res = yn * cosa_ref[...] + pltpu.roll(yn, 64, 1) * sina_ref[...]
                else:
                    res = (yn * cosb_ref[...] + pltpu.roll(yn, 96, 1) * sinbu_ref[...]
                           + pltpu.roll(yn, 32, 1) * sinbd_ref[...])
            if is_a:
                a1_ref[col] = res.astype(BF16)
                tmp_ref[col] = res
                for out_ref, dil in ((a4_ref, 4), (a16_ref, 16)):
                    for r in range(dil):
                        out_ref[col, r] = tmp_ref[col, pl.ds(r, bm // dil, stride=dil), :].astype(BF16)
            else:
                b_ref[col - N_COLS_A] = res.astype(BF16)


def _inproj(x2, g, w_all, layer, gain_cols, tables, batch, seq):
    t, d = x2.shape
    bm = _pick(seq, 256)
    nps = seq // bm
    n = w_all.shape[2]
    tab_spec = pl.BlockSpec((bm, HEAD_DIM), lambda i: (i % nps, 0))

    def rm_spec(dil):
        return pl.BlockSpec((N_COLS_A, None, dil, bm // dil, HEAD_DIM),
                            lambda i: (0, i // nps, 0, i % nps, 0))

    def rm_shape(dil):
        return jax.ShapeDtypeStruct((N_COLS_A, batch, dil, seq // dil, HEAD_DIM), BF16)

    return pl.pallas_call(
        _inproj_kernel,
        grid=(t // bm,),
        in_specs=[
            pl.BlockSpec((bm, d), lambda i: (i, 0)),
            pl.BlockSpec((1, d), lambda i: (0, 0)),
            pl.BlockSpec((None, d, n), lambda i: (layer, 0, 0), pipeline_mode=pl.Buffered(1)),
            pl.BlockSpec((1, n), lambda i: (0, 0)),
            tab_spec, tab_spec, tab_spec, tab_spec, tab_spec,
        ],
        out_specs=[pl.BlockSpec((N_COLS_A, bm, HEAD_DIM), lambda i: (0, i, 0)), rm_spec(4), rm_spec(16),
                   pl.BlockSpec((N_COLS_B, bm, HEAD_DIM), lambda i: (0, i, 0))],
        out_shape=[jax.ShapeDtypeStruct((N_COLS_A, t, HEAD_DIM), BF16), rm_shape(4), rm_shape(16),
                   jax.ShapeDtypeStruct((N_COLS_B, t, HEAD_DIM), BF16)],
        scratch_shapes=[pltpu.VMEM((bm, d), BF16), pltpu.VMEM((N_COLS_A, bm, HEAD_DIM), F32)],
        compiler_params=_params("parallel"),
        name="inproj",
    )(x2, g, w_all, gain_cols, *tables)


QB_DIL = 128
KW_DIL = QB_DIL + 2 * BAND_HALF
BAND_SHIFT = BAND_HALF.bit_length() - 1
assert 1 << BAND_SHIFT == BAND_HALF
N_WINDOW_PLACEMENTS = QB_DIL // BAND_HALF + 1
DIL_UNROLL = 8


def _dilated_kernel(q_ref, k_ref, v_ref, o_ref, lse_ref, bias_ref, *, seg, shifted):
    h = pl.program_id(1)
    rows = q_ref.shape[0]
    lane = lax.broadcasted_iota(jnp.int32, (QB_DIL, LANES), 1)

    @pl.when(h == 0)
    def _():
        lse_ref[...] = jnp.zeros_like(lse_ref)
        rel0 = (lax.broadcasted_iota(jnp.int32, (QB_DIL, KW_DIL), 1)
                - lax.broadcasted_iota(jnp.int32, (QB_DIL, KW_DIL), 0))
        for c in range(bias_ref.shape[0]):
            bias_ref[c] = jnp.where(jnp.abs(rel0 - BAND_HALF * c) <= BAND_HALF, 0.0, NEG)

    def body(blk, carry):
        l0 = pl.multiple_of(blk * QB_DIL, QB_DIL)
        seg0 = l0 & (-seg)
        w0 = pl.multiple_of(jnp.clip(l0 - BAND_HALF, seg0, seg0 + seg - KW_DIL), BAND_HALF)
        q = q_ref[pl.ds(l0, QB_DIL), :]
        k = k_ref[pl.ds(w0, KW_DIL), :]
        v = v_ref[pl.ds(w0, KW_DIL), :]
        s = lax.dot_general(q, k, (((1,), (1,)), ((), ())), preferred_element_type=F32)
        s = s + bias_ref[lax.shift_right_logical(l0 - w0, BAND_SHIFT)]
        if shifted:
            m = jnp.max(s, axis=-1, keepdims=True)
            p = jnp.exp2(s - m)
        else:
            p = jnp.exp2(s)
        den = jnp.sum(p, axis=-1, keepdims=True)
        o = jnp.dot(p.astype(BF16), v, preferred_element_type=F32) / den
        lse = jnp.log2(den) + m if shifted else jnp.log2(den)
        o_ref[pl.ds(l0, QB_DIL), :] = o.astype(BF16)
        lse_ref[pl.ds(l0, QB_DIL), :] = jnp.where(lane == h, lse, lse_ref[pl.ds(l0, QB_DIL), :])
        return carry

    lax.fori_loop(0, rows // QB_DIL, body, 0, unroll=DIL_UNROLL)


def _dilated_branch(qkv_rm, batch, seq, dil, shifted):
    seg = seq // dil
    assert seg >= KW_DIL and seg % QB_DIL == 0 and seg & (seg - 1) == 0

    def spec(off):
        return pl.BlockSpec((None, None, seq, HEAD_DIM), lambda b, h: (off + h, b, 0, 0))

    return pl.pallas_call(
        functools.partial(_dilated_kernel, seg=seg, shifted=shifted),
        grid=(batch, N_HEADS_A),
        in_specs=[spec(0), spec(N_HEADS_A), spec(2 * N_HEADS_A)],
        out_specs=[pl.BlockSpec((None, None, seq, HEAD_DIM), lambda b, h: (h, b, 0, 0)),
                   pl.BlockSpec((None, seq, LANES), lambda b, h: (b, 0, 0))],
        out_shape=[jax.ShapeDtypeStruct((N_HEADS_A, batch, seq, HEAD_DIM), BF16),
                   jax.ShapeDtypeStruct((batch, seq, LANES), F32)],
        scratch_shapes=[pltpu.VMEM((N_WINDOW_PLACEMENTS, QB_DIL, KW_DIL), F32)],
        compiler_params=_params("parallel", "arbitrary"),
        name=f"dilated_d{dil}" + ("_shifted" if shifted else ""),
    )(qkv_rm, qkv_rm, qkv_rm)


def _gqa_kernel(q_ref, k_ref, v_ref, o_ref, *, bk, shifted):
    rep, bq, dh = q_ref.shape
    seq = k_ref.shape[0]
    rows = rep * bq
    q = q_ref[...].reshape(rows, dh)

    def scores(c):
        c0 = pl.multiple_of(c * bk, bk)
        k = k_ref[pl.ds(c0, bk), :]
        v = v_ref[pl.ds(c0, bk), :]
        return lax.dot_general(q, k, (((1,), (1,)), ((), ())), preferred_element_type=F32), v

    if shifted:
        def body(c, carry):
            m, l, acc = carry
            s, v = scores(c)
            m_new = jnp.maximum(m, jnp.max(s, axis=-1, keepdims=True))
            alpha = jnp.exp2(m - m_new)
            p = jnp.exp2(s - m_new)
            l = alpha * l + jnp.sum(p, axis=-1, keepdims=True)
            acc = alpha * acc + jnp.dot(p.astype(BF16), v, preferred_element_type=F32)
            return m_new, l, acc

        init = (jnp.full((rows, 1), NEG, F32), jnp.zeros((rows, 1), F32), jnp.zeros((rows, dh), F32))
        _, l, acc = lax.fori_loop(0, seq // bk, body, init)
    else:
        def body(c, carry):
            lsum, acc = carry
            s, v = scores(c)
            p = jnp.exp2(s)
            for tile in range(bk // LANES):
                lsum = lsum + p[:, tile * LANES:(tile + 1) * LANES]
            acc = acc + jnp.dot(p.astype(BF16), v, preferred_element_type=F32)
            return lsum, acc

        init = (jnp.zeros((rows, LANES), F32), jnp.zeros((rows, dh), F32))
        lsum, acc = lax.fori_loop(0, seq // bk, body, init, unroll=4)
        l = jnp.sum(lsum, axis=-1, keepdims=True)
    o = acc / l
    for h in range(rep):
        o_ref[:, h * dh:(h + 1) * dh] = o[h * bq:(h + 1) * bq].astype(o_ref.dtype)


def _gqa(proj_b, batch, seq, shifted):
    t = batch * seq
    bq = _pick(seq, 512)
    bk = _pick(seq, 512)
    nq = seq // bq
    view = proj_b.reshape(N_COLS_B, batch, seq, HEAD_DIM)
    return pl.pallas_call(
        functools.partial(_gqa_kernel, bk=bk, shifted=shifted),
        grid=(batch, N_KV_B, nq),
        in_specs=[
            pl.BlockSpec((GQA_REP, None, bq, HEAD_DIM), lambda b, g, i: (g, b, i, 0)),
            pl.BlockSpec((None, None, seq, HEAD_DIM), lambda b, g, i: (N_HEADS_B + g, b, 0, 0)),
            pl.BlockSpec((None, None, seq, HEAD_DIM), lambda b, g, i: (N_HEADS_B + N_KV_B + g, b, 0, 0)),
        ],
        out_specs=pl.BlockSpec((bq, GQA_REP * HEAD_DIM), lambda b, g, i: (b * nq + i, g)),
        out_shape=jax.ShapeDtypeStruct((t, N_HEADS_B * HEAD_DIM), BF16),
        compiler_params=_params("parallel", "parallel", "arbitrary"),
        name="gqa" + ("_shifted" if shifted else ""),
    )(view, view, view)


def _outproj_kernel(o1_ref, o4_ref, o16_ref, l1_ref, l4_ref, l16_ref, ob_ref, ona_ref, onb_ref,
                    w_ref, x_ref, out_ref, mixed_ref, oa_ref, t4_ref, t16_ref):
    wa = N_HEADS_A * HEAD_DIM
    wb = ob_ref.shape[1]
    bm = oa_ref.shape[0]

    def to_token_order(tmp_ref, slot, load_residue, dil):
        for r in range(dil):
            tmp_ref[slot, pl.ds(r, bm // dil, stride=dil), :] = load_residue(r)
        return tmp_ref[slot]

    l1 = l1_ref[...]
    l4 = to_token_order(t4_ref, N_HEADS_A, lambda r: l4_ref[r], 4)
    l16 = to_token_order(t16_ref, N_HEADS_A, lambda r: l16_ref[r], 16)
    mx = jnp.maximum(jnp.maximum(l1, l4), l16)
    e1, e4, e16 = jnp.exp2(l1 - mx), jnp.exp2(l4 - mx), jnp.exp2(l16 - mx)
    den = e1 + e4 + e16
    w1, w4, w16 = e1 / den, e4 / den, e16 / den
    ss = jnp.zeros((bm, HEAD_DIM), F32)
    for h in range(N_HEADS_A):
        o4 = to_token_order(t4_ref, h, lambda r: o4_ref[h, r].astype(F32), 4)
        o16 = to_token_order(t16_ref, h, lambda r: o16_ref[h, r].astype(F32), 16)
        oa = (w1[:, h:h + 1] * o1_ref[h].astype(F32) + w4[:, h:h + 1] * o4
              + w16[:, h:h + 1] * o16)
        oa_ref[:, h * HEAD_DIM:(h + 1) * HEAD_DIM] = oa
        ss = ss + oa * oa
    ra = lax.rsqrt(jnp.sum(ss, axis=-1, keepdims=True) / wa + EPS)
    mixed_ref[:, :wa] = (oa_ref[...] * ra * ona_ref[...]).astype(BF16)
    ob = ob_ref[...].astype(F32)
    rb = lax.rsqrt(jnp.mean(ob * ob, axis=-1, keepdims=True) + EPS)
    mixed_ref[:, wa:wa + wb] = (ob * rb * onb_ref[...]).astype(BF16)
    out_ref[...] = x_ref[...] + jnp.dot(mixed_ref[...], w_ref[...], preferred_element_type=F32)


def _outproj(o_branches, lse_branches, ob, on_a, on_b, w_all, layer, x2, batch, seq):
    t, d = x2.shape
    bm = _pick(seq, 256)
    nps = seq // bm
    wa = N_HEADS_A * HEAD_DIM
    wb = ob.shape[1]
    o1, o4, o16 = o_branches
    l1, l4, l16 = lse_branches
    o1 = o1.reshape(N_HEADS_A, t, HEAD_DIM)
    l1 = l1.reshape(t, LANES)

    def o_spec(dil):
        return pl.BlockSpec((N_HEADS_A, None, dil, bm // dil, HEAD_DIM),
                            lambda i: (0, i // nps, 0, i % nps, 0))

    def l_spec(dil):
        return pl.BlockSpec((None, dil, bm // dil, LANES), lambda i: (i // nps, 0, i % nps, 0))

    def rm(a, dil):
        return a.reshape(a.shape[:-2] + (dil, seq // dil, a.shape[-1]))

    return pl.pallas_call(
        _outproj_kernel,
        grid=(t // bm,),
        in_specs=[
            pl.BlockSpec((N_HEADS_A, bm, HEAD_DIM), lambda i: (0, i, 0)), o_spec(4), o_spec(16),
            pl.BlockSpec((bm, LANES), lambda i: (i, 0)), l_spec(4), l_spec(16),
            pl.BlockSpec((bm, wb), lambda i: (i, 0)),
            pl.BlockSpec((1, wa), lambda i: (0, 0)),
            pl.BlockSpec((1, wb), lambda i: (0, 0)),
            pl.BlockSpec((None, wa + wb, d), lambda i: (layer, 0, 0), pipeline_mode=pl.Buffered(1)),
            pl.BlockSpec((bm, d), lambda i: (i, 0)),
        ],
        out_specs=pl.BlockSpec((bm, d), lambda i: (i, 0)),
        out_shape=jax.ShapeDtypeStruct((t, d), F32),
        scratch_shapes=[pltpu.VMEM((bm, wa + wb), BF16), pltpu.VMEM((bm, wa), F32),
                        pltpu.VMEM((N_HEADS_A + 1, bm, HEAD_DIM), F32),
                        pltpu.VMEM((N_HEADS_A + 1, bm, HEAD_DIM), F32)],
        compiler_params=_params("parallel"),
        name="outproj",
    )(o1, rm(o4, 4), rm(o16, 16), l1, rm(l4, 4), rm(l16, 16), ob, on_a, on_b, w_all, x2)


def _swiglu_step(h, wg_ref, wu_ref, wd_ref, acc_ref):
    g = jnp.dot(h, wg_ref[...], preferred_element_type=F32)
    u = jnp.dot(h, wu_ref[...], preferred_element_type=F32)
    a = (g * jax.nn.sigmoid(g) * u).astype(BF16)
    acc_ref[...] += jnp.dot(a, wd_ref[...], preferred_element_type=F32)


CAST_COLS = 512


def _cast_chunks(n_steps, f, rows_gu, rows_d):
    ncb = f // CAST_COLS
    assert f % CAST_COLS == 0 and n_steps % ncb == 0
    nrg = n_steps // ncb
    assert rows_gu % nrg == 0 and rows_d % n_steps == 0
    rpc, rpd = rows_gu // nrg, rows_d // n_steps
    assert rpc % 16 == 0 and rpd % 16 == 0
    return ncb, rpc, rpd


def _dense_ffn_kernel(x_ref, g_ref, wg_ref, wu_ref, wd_ref, eg_ref, eu_ref, ed_ref,
                      out_ref, og_ref, ou_ref, od_ref,
                      hn_ref, ing_ref, inu_ref, ind_ref, outg_ref, outu_ref, outd_ref,
                      sem_in, sem_out, *, li, ncb, rpc, rpd):
    i = pl.program_id(0)
    j = pl.program_id(1)
    nj = pl.num_programs(1)
    step = i * nj + j
    last = pl.num_programs(0) * nj - 1
    slot = step % 2

    def in_copies(s, sl):
        rg = s // ncb
        cb = s - rg * ncb
        r0 = pl.multiple_of(rg * rpc, 16)
        c0 = pl.multiple_of(cb * CAST_COLS, CAST_COLS)
        d0 = pl.multiple_of(s * rpd, 16)
        return (pltpu.make_async_copy(eg_ref.at[li, pl.ds(r0, rpc), pl.ds(c0, CAST_COLS)],
                                      ing_ref.at[sl], sem_in.at[sl, 0]),
                pltpu.make_async_copy(eu_ref.at[li, pl.ds(r0, rpc), pl.ds(c0, CAST_COLS)],
                                      inu_ref.at[sl], sem_in.at[sl, 1]),
                pltpu.make_async_copy(ed_ref.at[li, pl.ds(d0, rpd), :], ind_ref.at[sl], sem_in.at[sl, 2]))

    def out_copies(s, sl):
        rg = s // ncb
        cb = s - rg * ncb
        r0 = pl.multiple_of(rg * rpc, 16)
        c0 = pl.multiple_of(cb * CAST_COLS, CAST_COLS)
        d0 = pl.multiple_of(s * rpd, 16)
        return (pltpu.make_async_copy(outg_ref.at[sl], og_ref.at[pl.ds(r0, rpc), pl.ds(c0, CAST_COLS)],
                                      sem_out.at[sl, 0]),
                pltpu.make_async_copy(outu_ref.at[sl], ou_ref.at[pl.ds(r0, rpc), pl.ds(c0, CAST_COLS)],
                                      sem_out.at[sl, 1]),
                pltpu.make_async_copy(outd_ref.at[sl], od_ref.at[pl.ds(d0, rpd), :], sem_out.at[sl, 2]))

    @pl.when(step == 0)
    def _():
        for c in in_copies(step, slot):
            c.start()

    @pl.when(step < last)
    def _():
        for c in in_copies(step + 1, 1 - slot):
            c.start()

    @pl.when(step >= 2)
    def _():
        for c in out_copies(step - 2, slot):
            c.wait()

    @pl.when(j == 0)
    def _():
        _rms_to_bf16(x_ref, g_ref, hn_ref)
        out_ref[...] = jnp.zeros_like(out_ref)

    for c in in_copies(step, slot):
        c.wait()
    outg_ref[slot] = ing_ref[slot].astype(BF16)
    outu_ref[slot] = inu_ref[slot].astype(BF16)
    outd_ref[slot] = ind_ref[slot].astype(BF16)
    _swiglu_step(hn_ref[...], wg_ref, wu_ref, wd_ref, out_ref)
    for c in out_copies(step, slot):
        c.start()

    @pl.when(j == nj - 1)
    def _():
        out_ref[...] += x_ref[...]

    @pl.when(step == last)
    def _():
        for c in out_copies(step, slot):
            c.wait()

        @pl.when(step >= 1)
        def _():
            for c in out_copies(step - 1, 1 - slot):
                c.wait()


def _dense_ffn(x2, g, wg, wu, wd, li, eg, eu, ed):
    t, d = x2.shape
    f = wg.shape[2]
    n_layers, n_exp, de, fe = eg.shape
    bm = _pick(t, 1024)
    bf = _pick(f, 256)
    grid = (t // bm, f // bf)
    rows_gu, rows_d = n_exp * de, n_exp * fe
    ncb, rpc, rpd = _cast_chunks(grid[0] * grid[1], fe, rows_gu, rows_d)
    any_spec = pl.BlockSpec(memory_space=pl.ANY)
    out, og, ou, od = pl.pallas_call(
        functools.partial(_dense_ffn_kernel, li=li, ncb=ncb, rpc=rpc, rpd=rpd),
        grid=grid,
        in_specs=[
            pl.BlockSpec((bm, d), lambda i, j: (i, 0)),
            pl.BlockSpec((1, d), lambda i, j: (0, 0)),
            pl.BlockSpec((None, d, bf), lambda i, j: (li, 0, j)),
            pl.BlockSpec((None, d, bf), lambda i, j: (li, 0, j)),
            pl.BlockSpec((None, bf, d), lambda i, j: (li, j, 0)),
            any_spec, any_spec, any_spec,
        ],
        out_specs=[pl.BlockSpec((bm, d), lambda i, j: (i, 0)), any_spec, any_spec, any_spec],
        out_shape=[jax.ShapeDtypeStruct((t, d), F32),
                   jax.ShapeDtypeStruct((rows_gu, fe), BF16), jax.ShapeDtypeStruct((rows_gu, fe), BF16),
                   jax.ShapeDtypeStruct((rows_d, de), BF16)],
        scratch_shapes=[pltpu.VMEM((bm, d), BF16),
                        pltpu.VMEM((2, rpc, CAST_COLS), F32), pltpu.VMEM((2, rpc, CAST_COLS), F32),
                        pltpu.VMEM((2, rpd, de), F32),
                        pltpu.VMEM((2, rpc, CAST_COLS), BF16), pltpu.VMEM((2, rpc, CAST_COLS), BF16),
                        pltpu.VMEM((2, rpd, de), BF16),
                        pltpu.SemaphoreType.DMA((2, 3)), pltpu.SemaphoreType.DMA((2, 3))],
        compiler_params=_params("arbitrary", "arbitrary"),
        name="dense_ffn",
    )(x2, g, wg, wu, wd, eg.reshape(n_layers, rows_gu, fe), eu.reshape(n_layers, rows_gu, fe),
      ed.reshape(n_layers, rows_d, de))
    return (out, og.reshape(1, n_exp, de, fe), ou.reshape(1, n_exp, de, fe),
            od.reshape(1, n_exp, fe, de))


INFO_E1, INFO_E2, INFO_R1, INFO_R2, INFO_G1, INFO_G2 = range(6)


def _router_kernel(x_ref, g_ref, wr_ref, hn_ref, info_ref, cnt_ref, run_ref):
    i = pl.program_id(0)
    bm = x_ref.shape[0]

    @pl.when(i == 0)
    def _():
        run_ref[...] = jnp.zeros_like(run_ref)

    x = x_ref[...]
    ms = jnp.mean(x * x, axis=-1, keepdims=True)
    hn = x * lax.rsqrt(ms + EPS) * g_ref[...]
    hn_ref[...] = hn
    logits = jnp.dot(hn, wr_ref[...], preferred_element_type=F32, precision=lax.Precision.HIGHEST)
    lane = lax.broadcasted_iota(jnp.int32, (bm, LANES), 1)
    lg = jnp.where(lane < N_EXPERTS, logits, -jnp.inf)
    v1 = jnp.max(lg, axis=-1, keepdims=True)
    i1 = jnp.min(jnp.where(lg == v1, lane, LANES), axis=-1, keepdims=True)
    lg2 = jnp.where(lane == i1, -jnp.inf, lg)
    v2 = jnp.max(lg2, axis=-1, keepdims=True)
    i2 = jnp.min(jnp.where(lg2 == v2, lane, LANES), axis=-1, keepdims=True)
    e = jnp.exp(v2 - v1)
    g1 = 1.0 / (1.0 + e)
    g2 = e / (1.0 + e)
    hot1 = lane == i1
    hot2 = lane == i2
    onehot = (hot1 | hot2).astype(F32)
    tri = (lax.broadcasted_iota(jnp.int32, (bm, bm), 1)
           < lax.broadcasted_iota(jnp.int32, (bm, bm), 0)).astype(BF16)
    rank = jnp.dot(tri, onehot.astype(BF16), preferred_element_type=F32) + run_ref[0:1, :]
    r1 = jnp.sum(jnp.where(hot1, rank, 0.0), axis=-1, keepdims=True)
    r2 = jnp.sum(jnp.where(hot2, rank, 0.0), axis=-1, keepdims=True)
    run_ref[...] = run_ref[...] + jnp.sum(onehot, axis=0, keepdims=True)
    info = jnp.zeros((bm, LANES), F32)
    for col, val in ((INFO_E1, i1.astype(F32)), (INFO_E2, i2.astype(F32)), (INFO_R1, r1),
                     (INFO_R2, r2), (INFO_G1, g1), (INFO_G2, g2)):
        info = jnp.where(lane == col, val, info)
    info_ref[...] = info
    cnt_ref[...] = run_ref[...]


def _router(x2, g, wr_pad):
    t, d = x2.shape
    bm = _pick(t, 512)
    return pl.pallas_call(
        _router_kernel,
        grid=(t // bm,),
        in_specs=[
            pl.BlockSpec((bm, d), lambda i: (i, 0)),
            pl.BlockSpec((1, d), lambda i: (0, 0)),
            pl.BlockSpec((d, LANES), lambda i: (0, 0)),
        ],
        out_specs=[
            pl.BlockSpec((bm, d), lambda i: (i, 0)),
            pl.BlockSpec((bm, LANES), lambda i: (i, 0)),
            pl.BlockSpec((8, LANES), lambda i: (0, 0)),
        ],
        out_shape=[jax.ShapeDtypeStruct((t, d), F32), jax.ShapeDtypeStruct((t, LANES), F32),
                   jax.ShapeDtypeStruct((8, LANES), F32)],
        scratch_shapes=[pltpu.VMEM((8, LANES), F32)],
        compiler_params=_params("arbitrary"),
        name="router",
    )(x2, g, wr_pad)


def _scatter_kernel(pos1_ref, pos2_ref, hn_ref, hs_in_ref, hs_ref, sem):
    del hs_in_ref
    bt = hn_ref.shape[0]

    def copies(k):
        src = hn_ref.at[pl.ds(k, 1), :]
        return (pltpu.make_async_copy(src, hs_ref.at[pl.ds(pos1_ref[0, 0, k], 1), :], sem.at[0]),
                pltpu.make_async_copy(src, hs_ref.at[pl.ds(pos2_ref[0, 0, k], 1), :], sem.at[1]))

    def start(k, c):
        a, b = copies(k)
        a.start()
        b.start()
        return c

    def wait(k, c):
        a, b = copies(k)
        a.wait()
        b.wait()
        return c

    lax.fori_loop(0, bt, start, 0, unroll=DMA_UNROLL)
    lax.fori_loop(0, bt, wait, 0, unroll=DMA_UNROLL)


def _scatter_rows(hn, pos1, pos2, n_rows):
    t, d = hn.shape
    bt = _pick(t, 512)
    smem_spec = pl.BlockSpec((1, 1, bt), lambda i: (i, 0, 0), memory_space=pltpu.SMEM)
    any_spec = pl.BlockSpec(memory_space=pl.ANY)
    return pl.pallas_call(
        _scatter_kernel,
        grid=(t // bt,),
        in_specs=[smem_spec, smem_spec, pl.BlockSpec((bt, d), lambda i: (i, 0)), any_spec],
        out_specs=any_spec,
        out_shape=jax.ShapeDtypeStruct((n_rows, d), hn.dtype),
        scratch_shapes=[pltpu.SemaphoreType.DMA((2,))],
        input_output_aliases={3: 0},
        compiler_params=_params("arbitrary"),
        name="scatter_rows",
    )(pos1.reshape(t // bt, 1, bt), pos2.reshape(t // bt, 1, bt), hn, jnp.zeros((n_rows, d), hn.dtype))


def _grouped_ffn_kernel(be_ref, na_ref, hs_ref, wg_ref, wu_ref, wd_ref, out_ref, hb_ref):
    del be_ref
    i = pl.program_id(0)
    j = pl.program_id(1)
    active = i < na_ref[0]

    @pl.when(j == 0)
    def _():
        out_ref[...] = jnp.zeros_like(out_ref)

    @pl.when(active & (j == 0))
    def _():
        hb_ref[...] = hs_ref[...].astype(BF16)

    @pl.when(active)
    def _():
        _swiglu_step(hb_ref[...], wg_ref, wu_ref, wd_ref, out_ref)


def _grouped_ffn(hs, block_expert, n_active, wg, wu, wd, li, bm):
    p, d = hs.shape
    f = wg.shape[3]
    bf = _pick(f, 512)
    nf = f // bf

    def jj(i, j, na):
        return jnp.where(i < na[0], j, nf - 1)

    grid_spec = pltpu.PrefetchScalarGridSpec(
        num_scalar_prefetch=2,
        grid=(p // bm, nf),
        in_specs=[
            pl.BlockSpec((bm, d), lambda i, j, be, na: (jnp.maximum(jnp.minimum(i, na[0] - 1), 0), 0)),
            pl.BlockSpec((None, None, d, bf), lambda i, j, be, na: (li, be[i], 0, jj(i, j, na))),
            pl.BlockSpec((None, None, d, bf), lambda i, j, be, na: (li, be[i], 0, jj(i, j, na))),
            pl.BlockSpec((None, None, bf, d), lambda i, j, be, na: (li, be[i], jj(i, j, na), 0)),
        ],
        out_specs=pl.BlockSpec((bm, d), lambda i, j, be, na: (i, 0)),
        scratch_shapes=[pltpu.VMEM((bm, d), BF16)],
    )
    return pl.pallas_call(
        _grouped_ffn_kernel,
        grid_spec=grid_spec,
        out_shape=jax.ShapeDtypeStruct((p, d), F32),
        compiler_params=_params("arbitrary", "arbitrary"),
        name="grouped_ffn",
    )(block_expert, n_active, hs, wg, wu, wd)


def _combine_kernel(pos1_ref, pos2_ref, x_ref, info_ref, y_ref, out_ref, buf1_ref, buf2_ref, sem):
    bt = x_ref.shape[0]

    def copies(k):
        return (pltpu.make_async_copy(y_ref.at[pl.ds(pos1_ref[0, 0, k], 1), :],
                                      buf1_ref.at[pl.ds(k, 1), :], sem.at[0]),
                pltpu.make_async_copy(y_ref.at[pl.ds(pos2_ref[0, 0, k], 1), :],
                                      buf2_ref.at[pl.ds(k, 1), :], sem.at[1]))

    def start(k, c):
        a, b = copies(k)
        a.start()
        b.start()
        return c

    def wait(k, c):
        a, b = copies(k)
        a.wait()
        b.wait()
        return c

    lax.fori_loop(0, bt, start, 0, unroll=DMA_UNROLL)
    lax.fori_loop(0, bt, wait, 0, unroll=DMA_UNROLL)
    info = info_ref[...]
    g1 = info[:, INFO_G1:INFO_G1 + 1]
    g2 = info[:, INFO_G2:INFO_G2 + 1]
    out_ref[...] = x_ref[...] + g1 * buf1_ref[...] + g2 * buf2_ref[...]


def _combine(x2, info, y, pos1, pos2):
    t, d = x2.shape
    bt = _pick(t, 512)
    smem_spec = pl.BlockSpec((1, 1, bt), lambda i: (i, 0, 0), memory_space=pltpu.SMEM)
    return pl.pallas_call(
        _combine_kernel,
        grid=(t // bt,),
        in_specs=[
            smem_spec, smem_spec,
            pl.BlockSpec((bt, d), lambda i: (i, 0)),
            pl.BlockSpec((bt, LANES), lambda i: (i, 0)),
            pl.BlockSpec(memory_space=pl.ANY),
        ],
        out_specs=pl.BlockSpec((bt, d), lambda i: (i, 0)),
        out_shape=jax.ShapeDtypeStruct((t, d), F32),
        scratch_shapes=[pltpu.VMEM((bt, d), F32), pltpu.VMEM((bt, d), F32),
                        pltpu.SemaphoreType.DMA((2,))],
        compiler_params=_params("arbitrary"),
        name="combine",
    )(pos1.reshape(t // bt, 1, bt), pos2.reshape(t // bt, 1, bt), x2, info, y)


MOE_BLOCK_ROWS = 512


def _moe_ffn(x2, g, wr, wg, wu, wd, li):
    t, d = x2.shape
    bm = MOE_BLOCK_ROWS
    wr_pad = jnp.zeros((d, LANES), F32).at[:, :N_EXPERTS].set(wr)
    hn, info, cnt = _router(x2, g, wr_pad)
    e1 = info[:, INFO_E1].astype(jnp.int32)
    e2 = info[:, INFO_E2].astype(jnp.int32)
    r1 = info[:, INFO_R1].astype(jnp.int32)
    r2 = info[:, INFO_R2].astype(jnp.int32)
    counts = cnt[0, :N_EXPERTS].astype(jnp.int32)
    blocks_per_expert = (counts + bm - 1) // bm
    block_end = jnp.cumsum(blocks_per_expert)
    offsets = (block_end - blocks_per_expert) * bm
    pos1 = offsets[e1] + r1
    pos2 = offsets[e2] + r2
    n_blocks = (2 * t) // bm + N_EXPERTS
    n_active = block_end[-1:]
    ids = jnp.minimum(jnp.arange(n_blocks, dtype=jnp.int32), n_active[0] - 1)
    block_expert = jnp.sum((ids[:, None] >= block_end[None, :]).astype(jnp.int32), axis=1)
    block_expert = jnp.minimum(block_expert, N_EXPERTS - 1)
    hs = _scatter_rows(hn, pos1, pos2, n_blocks * bm)
    y = _grouped_ffn(hs, block_expert, n_active.astype(jnp.int32), wg, wu, wd, li, bm)
    return _combine(x2, info, y, pos1, pos2)


def _attention(proj_a, proj_b, batch, seq, shifted):
    views = [a.reshape(N_COLS_A, batch, seq, HEAD_DIM) for a in proj_a]
    branches = [_dilated_branch(v, batch, seq, dil, shifted) for v, dil in zip(views, DILATIONS)]
    ob = _gqa(proj_b, batch, seq, shifted)
    return [b[0] for b in branches], [b[1] for b in branches], ob


def kernel(x, g_mix, w_in, qn_a, kn_a, qn_b, kn_b, on_a, on_b, w_out, g_ffn, w_gate, w_up, w_down,
           w_router, we_gate, we_up, we_down):
    batch, seq, d = x.shape
    depth = w_in.shape[0]
    t = batch * seq
    assert all(w // (2 * dil) == BAND_HALF for w, dil in DILATED_PATTERNS) and DILATIONS == (1, 4, 16)
    assert w_in.shape[2] == (N_COLS_A + N_COLS_B) * HEAD_DIM
    scale = HEAD_DIM ** -0.5 * LOG2_E
    tables = _rope_tables(seq)
    w_in_b, w_out_b = w_in.astype(BF16), w_out.astype(BF16)
    w_gate_b, w_up_b, w_down_b = w_gate.astype(BF16), w_up.astype(BF16), w_down.astype(BF16)
    assert depth % 2 == 0
    x2 = x.reshape(t, d)
    for l in range(depth):
        gains = jnp.concatenate([jnp.tile(qn_a[l] * scale, N_HEADS_A), jnp.tile(kn_a[l], N_HEADS_A),
                                 jnp.ones((N_HEADS_A * HEAD_DIM,), F32),
                                 jnp.tile(qn_b[l] * scale, N_HEADS_B), jnp.tile(kn_b[l], N_KV_B),
                                 jnp.ones((N_KV_B * HEAD_DIM,), F32)])[None, :]
        bound = 1.02 * HEAD_DIM ** 0.5 * jnp.maximum(
            jnp.max(jnp.abs(qn_a[l])) * jnp.max(jnp.abs(kn_a[l])),
            jnp.max(jnp.abs(qn_b[l])) * jnp.max(jnp.abs(kn_b[l])))
        a1, a4, a16, proj_b = _inproj(x2, g_mix[l][None, :], w_in_b, l, gains,
                                      tables[0] + tables[1], batch, seq)
        attend = functools.partial(_attention, (a1, a4, a16), proj_b, batch, seq)
        o_br, lse_br, ob = lax.cond(bound <= MAX_UNSHIFTED_SCORE, lambda: attend(False),
                                    lambda: attend(True))
        x2 = _outproj(o_br, lse_br, ob, on_a[l][None, :], on_b[l][None, :], w_out_b, l, x2, batch, seq)
        i = l // 2
        if l % 2 == 0:
            x2, *experts_b = _dense_ffn(x2, g_ffn[l][None, :], w_gate_b, w_up_b, w_down_b, i,
                                        we_gate, we_up, we_down)
        else:
            x2 = _moe_ffn(x2, g_ffn[l][None, :], w_router[i], *experts_b, 0)
    return x2.reshape(batch, seq, d)
```
